```python
import jax
import jax.numpy as jnp
from jax import lax
import numpy as np


D_MODEL = 1024
BATCH = 4
SEQ = 8192
DEPTH = 4

GRID_W = 64
CTX_LEN = 256
N_MIXERS = 3
EPS = 1e-6
ROPE_THETA = 10000.0

ATT_HEADS = 8
ATT_KV_HEADS = 2
ATT_HEAD_DIM = 128
Q_BLOCK = 128

GDN_QK_HEADS = 8
GDN_V_HEADS = 16
GDN_HEAD_DIM = 128
GDN_CONV_W = 5
GDN_CHUNK = 64

RET_HEADS = 8
RET_QK_DIM = 128
RET_V_DIM = 256
RET_CHUNK = 64

D_FF = 2816
FFN_CONV_W = 3

kernel_name = 'hybrid_gqa_deltanet_retention_dit'

F32 = jnp.float32


def _rms(x):
    xf = x.astype(F32)
    return xf * lax.rsqrt(jnp.mean(xf * xf, axis=-1, keepdims=True) + EPS)


def rms_norm(x, g):
    return (_rms(x) * g.astype(F32)).astype(x.dtype)


def l2_norm(x):
    xf = x.astype(F32)
    return (xf * lax.rsqrt(jnp.sum(xf * xf, axis=-1, keepdims=True) + EPS)).astype(x.dtype)


def _flip_seq(t):
    return jnp.flip(t, axis=2)


def axial_rope_tables(n_rows, head_dim):
    row = jnp.repeat(jnp.arange(n_rows, dtype=F32), GRID_W)
    col = jnp.tile(jnp.arange(GRID_W, dtype=F32), n_rows)
    axis_dim = head_dim // 2
    inv_freq = ROPE_THETA ** (-jnp.arange(0, axis_dim, 2, dtype=F32) / axis_dim)
    ang = jnp.concatenate([row[:, None] * inv_freq, col[:, None] * inv_freq], axis=-1)
    return jnp.cos(ang), jnp.sin(ang)


def apply_axial_rope(x, cos, sin):
    dh = x.shape[-1]
    q4 = dh // 4
    xf = x.astype(F32)

    def rot(t, cs, sn):
        t1, t2 = t[..., :q4], t[..., q4:]
        return jnp.concatenate([t1 * cs - t2 * sn, t2 * cs + t1 * sn], axis=-1)

    out = jnp.concatenate([rot(xf[..., :dh // 2], cos[:, :q4], sin[:, :q4]),
                           rot(xf[..., dh // 2:], cos[:, q4:], sin[:, q4:])], axis=-1)
    return out.astype(x.dtype)


def depthwise_conv(x, w):
    k_w, ch = w.shape
    pad = k_w // 2
    return lax.conv_general_dilated(x, w[:, None, :].astype(x.dtype), window_strides=(1,),
                                    padding=[(pad, pad)], dimension_numbers=('NWC', 'WIO', 'NWC'),
                                    feature_group_count=ch)


def attention_mixer(h, hc, w_in, q_g, k_g, w_out, rope, want_ctx):
    H, KV, dh = ATT_HEADS, ATT_KV_HEADS, ATT_HEAD_DIM
    G = H // KV
    B, n, _ = h.shape

    def project(t):
        L = t.shape[1]
        q, k, v = jnp.split(t @ w_in, [H * dh, (H + KV) * dh], axis=-1)
        q = rms_norm(q.reshape(B, L, H, dh), q_g).transpose(0, 2, 1, 3)
        k = rms_norm(k.reshape(B, L, KV, dh), k_g).transpose(0, 2, 1, 3)
        v = v.reshape(B, L, KV, dh).transpose(0, 2, 1, 3)
        return q, k, v

    q, k, v = project(h)
    qc, kc, vc = project(hc)
    cos, sin = rope
    q = apply_axial_rope(q, cos, sin)
    k = apply_axial_rope(k, cos, sin)
    k_all = jnp.concatenate([k, kc], axis=2)
    v_all = jnp.concatenate([v, vc], axis=2)
    scale = dh ** -0.5

    def attend(qg, keys, vals):
        s = jnp.einsum('bkgqd,bksd->bkgqs', qg, keys).astype(F32) * scale
        p = jax.nn.softmax(s, axis=-1).astype(vals.dtype)
        return jnp.einsum('bkgqs,bksd->bkgqd', p, vals)

    nblk = n // Q_BLOCK
    qb = q.reshape(B, KV, G, nblk, Q_BLOCK, dh).transpose(3, 0, 1, 2, 4, 5)
    o = lax.map(lambda qi: attend(qi, k_all, v_all), qb)
    o = o.transpose(1, 0, 4, 2, 3, 5).reshape(B, n, H * dh)
    y = o @ w_out
    if not want_ctx:
        return y, None
    Lc = hc.shape[1]
    oc = attend(qc.reshape(B, KV, G, Lc, dh), kc, vc)
    oc = oc.transpose(0, 3, 1, 2, 4).reshape(B, Lc, H * dh)
    return y, oc @ w_out


def gated_delta_chunked(q, k, v, g, beta, S0):
    B, H, L, dk = q.shape
    dv = v.shape[-1]
    C = GDN_CHUNK
    N = L // C
    qf = q.astype(F32).reshape(B, H, N, C, dk) * (dk ** -0.5)
    kf = k.astype(F32).reshape(B, H, N, C, dk)
    vf = v.astype(F32).reshape(B, H, N, C, dv)
    gc = jnp.cumsum(g.astype(F32).reshape(B, H, N, C), axis=-1)
    bt = beta.astype(F32).reshape(B, H, N, C)[..., None]
    incl = jnp.tril(jnp.ones((C, C), dtype=bool))
    eye = jnp.eye(C, dtype=F32)
    decay = jnp.exp(jnp.where(incl, gc[..., :, None] - gc[..., None, :], -jnp.inf))
    kb = kf * bt
    A = jnp.einsum('bhncd,bhnsd->bhncs', kb, kf) * decay * (1.0 - eye)
    M = A + eye
    u = lax.linalg.triangular_solve(M, vf * bt, left_side=True, lower=True, unit_diagonal=True)
    w = lax.linalg.triangular_solve(M, kb * jnp.exp(gc)[..., None], left_side=True, lower=True,
                                    unit_diagonal=True)
    qk = jnp.einsum('bhncd,bhnsd->bhncs', qf, kf) * decay
    qd = qf * jnp.exp(gc)[..., None]
    kd = kf * jnp.exp(gc[..., -1:] - gc)[..., None]
    g_last = jnp.exp(gc[..., -1])

    def step(S, inp):
        u_i, w_i, qd_i, kd_i, qk_i, gl_i = inp
        v_new = u_i - jnp.einsum('bhck,bhkv->bhcv', w_i, S)
        o = jnp.einsum('bhck,bhkv->bhcv', qd_i, S) + jnp.einsum('bhcs,bhsv->bhcv', qk_i, v_new)
        S = S * gl_i[..., None, None] + jnp.einsum('bhck,bhcv->bhkv', kd_i, v_new)
        return S, o

    xs = (jnp.moveaxis(u, 2, 0), jnp.moveaxis(w, 2, 0), jnp.moveaxis(qd, 2, 0),
          jnp.moveaxis(kd, 2, 0), jnp.moveaxis(qk, 2, 0), jnp.moveaxis(g_last, 2, 0))
    S, o = lax.scan(step, S0, xs)
    o = jnp.moveaxis(o, 0, 2).reshape(B, H, L, dv)
    return o.astype(v.dtype), S


def gdn_mixer(h, hc, w_in, conv_w, A_log, dt_bias, norm_g, w_out, want_ctx):
    Hk, Hv, dh = GDN_QK_HEADS, GDN_V_HEADS, GDN_HEAD_DIM
    qk_w, v_w = Hk * dh, Hv * dh
    rep = Hv // Hk
    B = h.shape[0]

    def project(t):
        L = t.shape[1]
        qkv, z, a, b = jnp.split(t @ w_in, [2 * qk_w + v_w, 2 * qk_w + 2 * v_w,
                                            2 * qk_w + 2 * v_w + 2 * Hv], axis=-1)
        qkv = jax.nn.silu(depthwise_conv(qkv, conv_w))
        q, k, v = jnp.split(qkv, [qk_w, 2 * qk_w], axis=-1)
        q = jnp.repeat(l2_norm(q.reshape(B, L, Hk, dh)).transpose(0, 2, 1, 3), rep, axis=1)
        k = jnp.repeat(l2_norm(k.reshape(B, L, Hk, dh)).transpose(0, 2, 1, 3), rep, axis=1)
        v = v.reshape(B, L, Hv, dh).transpose(0, 2, 1, 3)
        g = -jnp.exp(A_log.astype(F32)) * jax.nn.softplus(a.astype(F32).reshape(B, L, 2, Hv)
                                                          + dt_bias.astype(F32))
        beta = jax.nn.sigmoid(b.astype(F32).reshape(B, L, 2, Hv))
        return q, k, v, z, g.transpose(2, 0, 3, 1), beta.transpose(2, 0, 3, 1)

    q, k, v, z, g, beta = project(h)
    qc, kc, vc, zc, gc, bc = project(hc)
    S0 = jnp.zeros((B, Hv, dh, dh), F32)
    fl = _flip_seq
    oc_f, sc_f = gated_delta_chunked(qc, kc, vc, gc[0], bc[0], S0)
    oc_b, sc_b = gated_delta_chunked(fl(qc), fl(kc), fl(vc), fl(gc[1]), fl(bc[1]), S0)
    o_f, _ = gated_delta_chunked(q, k, v, g[0], beta[0], sc_f)
    o_b, _ = gated_delta_chunked(fl(q), fl(k), fl(v), fl(g[1]), fl(beta[1]), sc_b)

    def out(o, zz):
        L = o.shape[2]
        o = rms_norm(o.transpose(0, 2, 1, 3), norm_g) * jax.nn.silu(zz.reshape(B, L, Hv, dh))
        return o.reshape(B, L, Hv * dh) @ w_out

    y = out(o_f + fl(o_b), z)
    if not want_ctx:
        return y, None
    return y, out(oc_f + fl(oc_b), zc)


def retention_chunked(q, k, v, log_gamma, S0):
    B, H, L, dk = q.shape
    dv = v.shape[-1]
    C = RET_CHUNK
    N = L // C
    qf = q.astype(F32).reshape(B, H, N, C, dk) * (dk ** -0.5)
    kf = k.astype(F32).reshape(B, H, N, C, dk)
    vf = v.astype(F32).reshape(B, H, N, C, dv)
    lg = log_gamma.astype(F32)
    pos = jnp.arange(C, dtype=F32)
    diff = pos[:, None] - pos[None, :]
    intra = jnp.exp(jnp.where(diff >= 0, diff[None] * lg[:, None, None], -jnp.inf))
    q_dec = jnp.exp((pos + 1.0)[None, :] * lg[:, None])
    k_dec = jnp.exp((C - 1.0 - pos)[None, :] * lg[:, None])
    chunk_dec = jnp.exp(C * lg)[:, None, None]
    scores = jnp.einsum('bhncd,bhnsd->bhncs', qf, kf) * intra[:, None]
    o_intra = jnp.einsum('bhncs,bhnsv->bhncv', scores, vf)
    qd = qf * q_dec[:, None, :, None]
    kd = kf * k_dec[:, None, :, None]

    def step(S, inp):
        qd_i, kd_i, v_i = inp
        o = jnp.einsum('bhcd,bhdv->bhcv', qd_i, S)
        S = S * chunk_dec + jnp.einsum('bhcd,bhcv->bhdv', kd_i, v_i)
        return S, o

    S, o_inter = lax.scan(step, S0, (jnp.moveaxis(qd, 2, 0), jnp.moveaxis(kd, 2, 0),
                                     jnp.moveaxis(vf, 2, 0)))
    o = (o_intra + jnp.moveaxis(o_inter, 0, 2)).reshape(B, H, L, dv)
    return o.astype(v.dtype), S


def retention_mixer(h, hc, w_in, decay, w_out, rope, want_ctx):
    H, dk, dv = RET_HEADS, RET_QK_DIM, RET_V_DIM
    B = h.shape[0]

    def project(t):
        L = t.shape[1]
        q, k, v, gate = jnp.split(t @ w_in, [H * dk, 2 * H * dk, 2 * H * dk + H * dv], axis=-1)
        q = q.reshape(B, L, H, dk).transpose(0, 2, 1, 3)
        k = k.reshape(B, L, H, dk).transpose(0, 2, 1, 3)
        v = v.reshape(B, L, H, dv).transpose(0, 2, 1, 3)
        return q, k, v, gate

    q, k, v, gate = project(h)
    qc, kc, vc, gate_c = project(hc)
    cos, sin = rope
    q = apply_axial_rope(q, cos, sin)
    k = apply_axial_rope(k, cos, sin)
    log_gamma = -decay.astype(F32)
    S0 = jnp.zeros((B, H, dk, dv), F32)
    fl = _flip_seq
    oc_f, sc_f = retention_chunked(qc, kc, vc, log_gamma[0], S0)
    oc_b, sc_b = retention_chunked(fl(qc), fl(kc), fl(vc), log_gamma[1], S0)
    o_f, _ = retention_chunked(q, k, v, log_gamma[0], sc_f)
    o_b, _ = retention_chunked(fl(q), fl(k), fl(v), log_gamma[1], sc_b)

    def out(o, gt):
        L = o.shape[2]
        o = _rms(o.transpose(0, 2, 1, 3)) * jax.nn.silu(gt.astype(F32).reshape(B, L, H, dv))
        return o.reshape(B, L, H * dv).astype(h.dtype) @ w_out

    y = out(o_f + fl(o_b), gate)
    if not want_ctx:
        return y, None
    return y, out(oc_f + fl(oc_b), gate_c)


def conv_ffn(h, w_in, conv_w, conv_b, w_out):
    val, gate = jnp.split(h @ w_in, 2, axis=-1)
    gate = depthwise_conv(gate, conv_w) + conv_b
    return (jax.nn.gelu(gate, approximate=False) * val) @ w_out


def setup_inputs(seed: int = 0) -> dict:
    key = jax.random.key(seed)
    ks = list(jax.random.split(key, 32))
    n_att = len(range(0, DEPTH, N_MIXERS))
    n_gdn = len(range(1, DEPTH, N_MIXERS))
    n_ret = len(range(2, DEPTH, N_MIXERS))
    D = D_MODEL

    def normal(shape, std=1.0):
        return jax.random.normal(ks.pop(), shape, F32) * std

    def dense(shape, fan_in, scale=1.0):
        return normal(shape, scale * fan_in ** -0.5)

    def gain(shape):
        return 1.0 + normal(shape, 0.05)

    att_in = (ATT_HEADS + 2 * ATT_KV_HEADS) * ATT_HEAD_DIM
    gdn_in = 2 * GDN_QK_HEADS * GDN_HEAD_DIM + 2 * GDN_V_HEADS * GDN_HEAD_DIM + 4 * GDN_V_HEADS
    ret_in = 2 * RET_HEADS * RET_QK_DIM + 2 * RET_HEADS * RET_V_DIM

    A_init = jax.random.uniform(ks.pop(), (n_gdn, 2, GDN_V_HEADS), F32, 1.0, 16.0)
    dt = jnp.exp(jax.random.uniform(ks.pop(), (n_gdn, 2, GDN_V_HEADS), F32,
                                    float(np.log(1e-3)), float(np.log(1e-1))))
    ret_base = -jnp.log(1.0 - 2.0 ** (-5.0 - jnp.arange(RET_HEADS, dtype=F32)))

    return {
        'x': normal((BATCH, SEQ, D)),
        'c': normal((BATCH, D)),
        'ctx': normal((BATCH, CTX_LEN, D)),
        'c_ctx': normal((D,)),
        'mod_w': dense((DEPTH, D, 6 * D), D, 0.5),
        'mod_b': normal((DEPTH, 6 * D), 0.02),
        'norm1_g': gain((DEPTH, D)),
        'norm2_g': gain((DEPTH, D)),
        'att_w_in': dense((n_att, D, att_in), D),
        'att_q_g': gain((n_att, ATT_HEAD_DIM)),
        'att_k_g': gain((n_att, ATT_HEAD_DIM)),
        'att_w_out': dense((n_att, ATT_HEADS * ATT_HEAD_DIM, D), ATT_HEADS * ATT_HEAD_DIM),
        'gdn_w_in': dense((n_gdn, D, gdn_in), D),
        'gdn_conv_w': dense((n_gdn, GDN_CONV_W, (2 * GDN_QK_HEADS + GDN_V_HEADS) * GDN_HEAD_DIM),
                            GDN_CONV_W),
        'gdn_A_log': jnp.log(A_init),
        'gdn_dt_bias': dt + jnp.log(-jnp.expm1(-dt)),
        'gdn_norm_g': gain((n_gdn, GDN_HEAD_DIM)),
        'gdn_w_out': dense((n_gdn, GDN_V_HEADS * GDN_HEAD_DIM, D), GDN_V_HEADS * GDN_HEAD_DIM),
        'ret_w_in': dense((n_ret, D, ret_in), D),
        'ret_decay': ret_base * jnp.exp(normal((n_ret, 2, RET_HEADS), 0.05)),
        'ret_w_out': dense((n_ret, RET_HEADS * RET_V_DIM, D), RET_HEADS * RET_V_DIM),
        'ffn_w_in': dense((DEPTH, D, 2 * D_FF), D),
        'ffn_conv_w': dense((DEPTH, FFN_CONV_W, D_FF), FFN_CONV_W),
        'ffn_conv_b': normal((DEPTH, D_FF), 0.02),
        'ffn_w_out': dense((DEPTH, D_FF, D), D_FF),
        'final_g': gain((D,)),
    }


def reference(x, c, ctx, c_ctx, mod_w, mod_b, norm1_g, norm2_g,
              att_w_in, att_q_g, att_k_g, att_w_out,
              gdn_w_in, gdn_conv_w, gdn_A_log, gdn_dt_bias, gdn_norm_g, gdn_w_out,
              ret_w_in, ret_decay, ret_w_out,
              ffn_w_in, ffn_conv_w, ffn_conv_b, ffn_w_out, final_g):
    n = x.shape[1]
    rows = n // GRID_W
    rope_att = axial_rope_tables(rows, ATT_HEAD_DIM)
    rope_ret = axial_rope_tables(rows, RET_QK_DIM)
    silu_c = jax.nn.silu(c)
    silu_cc = jax.nn.silu(c_ctx)
    xc = ctx
    for i in range(DEPTH):
        last = i == DEPTH - 1
        mod = (silu_c @ mod_w[i] + mod_b[i])[:, None, :]
        modc = silu_cc @ mod_w[i] + mod_b[i]
        sh1, sc1, ga1, sh2, sc2, ga2 = jnp.split(mod, 6, axis=-1)
        csh1, csc1, cga1, csh2, csc2, cga2 = jnp.split(modc, 6, axis=-1)

        h = rms_norm(x, norm1_g[i]) * (1.0 + sc1) + sh1
        hc = rms_norm(xc, norm1_g[i]) * (1.0 + csc1) + csh1
        kind, j = i % N_MIXERS, i // N_MIXERS
        if kind == 0:
            y, yc = attention_mixer(h, hc, att_w_in[j], att_q_g[j], att_k_g[j], att_w_out[j],
                                    rope_att, not last)
        elif kind == 1:
            y, yc = gdn_mixer(h, hc, gdn_w_in[j], gdn_conv_w[j], gdn_A_log[j], gdn_dt_bias[j],
                              gdn_norm_g[j], gdn_w_out[j], not last)
        else:
            y, yc = retention_mixer(h, hc, ret_w_in[j], ret_decay[j], ret_w_out[j],
                                    rope_ret, not last)
        x = x + ga1 * y
        h = rms_norm(x, norm2_g[i]) * (1.0 + sc2) + sh2
        x = x + ga2 * conv_ffn(h, ffn_w_in[i], ffn_conv_w[i], ffn_conv_b[i], ffn_w_out[i])
        if not last:
            xc = xc + cga1 * yc
            hc = rms_norm(xc, norm2_g[i]) * (1.0 + csc2) + csh2
            xc = xc + cga2 * conv_ffn(hc, ffn_w_in[i], ffn_conv_w[i], ffn_conv_b[i], ffn_w_out[i])
    return rms_norm(x, final_g)
```

```python
import functools

import jax
import jax.numpy as jnp
from jax import lax
from jax.experimental import pallas as pl
from jax.experimental.pallas import tpu as pltpu

F32 = jnp.float32
BF16 = jnp.bfloat16

EPS = 1e-6
ROPE_THETA = 10000.0
GRID_W = 64
N_MIXERS = 3

ATT_HEADS, ATT_KV_HEADS, ATT_HEAD_DIM = 8, 2, 128
GDN_QK_HEADS, GDN_V_HEADS, GDN_HEAD_DIM, GDN_CONV_W = 8, 16, 128, 5
RET_HEADS, RET_QK_DIM, RET_V_DIM = 8, 128, 256
FFN_CONV_W = 3

LANES = 128
BF16_ROWS = 16
GDN_CHUNK = 128
GDN_INV_BLOCK = 16
RET_CHUNK = 256
VMEM_LIMIT = 56 * 1024 * 1024


def _cparams(*sem):
    return pltpu.CompilerParams(dimension_semantics=sem, vmem_limit_bytes=VMEM_LIMIT)


def _sigmoid(x):
    return 1.0 / (1.0 + jnp.exp(-x))


def _silu(x):
    return x * _sigmoid(x)


def _dot(a, b):
    return jnp.dot(a.astype(BF16), b.astype(BF16), preferred_element_type=F32)


def _dot_nt(a, b):
    return lax.dot_general(a.astype(BF16), b.astype(BF16), (((1,), (1,)), ((), ())),
                           preferred_element_type=F32)


def _mod_kernel(cc_ref, w_ref, b_ref, o_ref):
    s = _silu(cc_ref[...])
    o_ref[0] = _dot(s, w_ref[0]) + b_ref[0]


def _modulation(cc, mod_w, mod_b):
    depth, d, n6 = mod_w.shape
    tn = n6 // 4
    return pl.pallas_call(
        _mod_kernel,
        grid=(depth, n6 // tn),
        in_specs=[pl.BlockSpec((8, d), lambda l, j: (0, 0)),
                  pl.BlockSpec((1, d, tn), lambda l, j: (l, 0, j)),
                  pl.BlockSpec((1, 1, tn), lambda l, j: (l, 0, j))],
        out_specs=pl.BlockSpec((1, 8, tn), lambda l, j: (l, 0, j)),
        out_shape=jax.ShapeDtypeStruct((depth, 8, n6), F32),
        compiler_params=_cparams("parallel", "parallel"),
        name="modulation",
    )(cc, mod_w, mod_b.reshape(depth, 1, n6))


def _norm_mod(x, g, sc, sh):
    ms = jnp.mean(x * x, axis=-1, keepdims=True)
    return (x * lax.rsqrt(ms + EPS)) * g * (1.0 + sc) + sh


def _proj_kernel(x_ref, g_ref, sc_ref, sh_ref, w_ref, o_ref, h_ref):
    @pl.when(pl.program_id(2) == 0)
    def _():
        h_ref[...] = _norm_mod(x_ref[0], g_ref[...], sc_ref[0], sh_ref[0]).astype(BF16)

    o_ref[0] = jnp.dot(h_ref[...], w_ref[...], preferred_element_type=F32).astype(o_ref.dtype)


def _proj(x, g, sc, sh, w, out_dtype, tm, tn):
    b, l, d = x.shape
    n = w.shape[1]
    tm = min(tm, l)
    tn = min(tn, n)
    return pl.pallas_call(
        _proj_kernel,
        grid=(b, l // tm, n // tn),
        in_specs=[pl.BlockSpec((1, tm, d), lambda bi, i, j: (bi, i, 0)),
                  pl.BlockSpec((1, d), lambda bi, i, j: (0, 0)),
                  pl.BlockSpec((1, 1, d), lambda bi, i, j: (bi, 0, 0)),
                  pl.BlockSpec((1, 1, d), lambda bi, i, j: (bi, 0, 0)),
                  pl.BlockSpec((d, tn), lambda bi, i, j: (0, j))],
        out_specs=pl.BlockSpec((1, tm, tn), lambda bi, i, j: (bi, i, j)),
        out_shape=jax.ShapeDtypeStruct((b, l, n), out_dtype),
        scratch_shapes=[pltpu.VMEM((tm, d), BF16)],
        compiler_params=_cparams("parallel", "parallel", "arbitrary"),
        name="norm_mod_proj",
    )(x, g.reshape(1, d), sc, sh, w)


def _rope_tables(n, head_dim):
    rows = n // GRID_W
    row = jnp.repeat(jnp.arange(rows, dtype=F32), GRID_W)
    col = jnp.tile(jnp.arange(GRID_W, dtype=F32), rows)
    axis_dim = head_dim // 2
    inv_freq = ROPE_THETA ** (-jnp.arange(0, axis_dim, 2, dtype=F32) / axis_dim)
    ar, ac = row[:, None] * inv_freq, col[:, None] * inv_freq
    cr, sr, cc, sc = jnp.cos(ar), jnp.sin(ar), jnp.cos(ac), jnp.sin(ac)
    z = jnp.zeros_like(sr)
    return (jnp.concatenate([cr, cr, cc, cc], axis=-1),
            jnp.concatenate([-sr, z, -sc, z], axis=-1),
            jnp.concatenate([z, sr, z, sc], axis=-1))


def _head_prep_kernel(*refs, norm, rope):
    if rope:
        x_ref, g_ref, c_ref, sa_ref, sb_ref, o_ref = refs
    else:
        x_ref, g_ref, o_ref = refs
    x = x_ref[0].astype(F32)
    if norm:
        x = x * lax.rsqrt(jnp.mean(x * x, axis=-1, keepdims=True) + EPS)
    x = x * g_ref[0]
    if rope:
        x = (x * c_ref[...] + pltpu.roll(x, 96, 1) * sa_ref[...]
             + pltpu.roll(x, 32, 1) * sb_ref[...])
    o_ref[0] = x.astype(o_ref.dtype)


def _head_prep(x, gains, tables, norm, tm=1024):
    b, l, _ = x.shape
    nb = gains.shape[0]
    tm = min(tm, l)
    in_specs = [pl.BlockSpec((1, tm, LANES), lambda bi, i, c: (bi, i, c)),
                pl.BlockSpec((1, 1, LANES), lambda bi, i, c: (c, 0, 0))]
    args = [x, gains.reshape(nb, 1, LANES)]
    if tables is not None:
        in_specs += [pl.BlockSpec((tm, LANES), lambda bi, i, c: (i, 0))] * 3
        args += list(tables)
    return pl.pallas_call(
        functools.partial(_head_prep_kernel, norm=norm, rope=tables is not None),
        grid=(b, l // tm, nb),
        in_specs=in_specs,
        out_specs=pl.BlockSpec((1, tm, LANES), lambda bi, i, c: (bi, i, c)),
        out_shape=jax.ShapeDtypeStruct((b, l, nb * LANES), BF16),
        compiler_params=_cparams("parallel", "parallel", "parallel"),
        name="head_prep",
    )(*args)


def _attn_kernel(q_ref, k_ref, v_ref, o_ref, *, groups, tq, tk):
    dh = LANES
    q = jnp.concatenate([q_ref[0, :, g * dh:(g + 1) * dh] for g in range(groups)], axis=0)
    rows = groups * tq
    nk = k_ref.shape[1] // tk

    def body(i, carry):
        m, l, acc = carry
        off = pl.multiple_of(i * tk, tk)
        k = k_ref[0, pl.ds(off, tk), :]
        v = v_ref[0, pl.ds(off, tk), :]
        s = lax.dot_general(q, k, (((1,), (1,)), ((), ())), preferred_element_type=F32)
        m_new = jnp.maximum(m, jnp.max(s, axis=-1, keepdims=True))
        alpha = jnp.exp(m - m_new)
        p = jnp.exp(s - m_new)
        l = alpha * l + jnp.sum(p, axis=-1, keepdims=True)
        acc = alpha * acc + jnp.dot(p.astype(BF16), v, preferred_element_type=F32)
        return m_new, l, acc

    m0 = jnp.full((rows, 1), -1e30, F32)
    l0 = jnp.zeros((rows, 1), F32)
    a0 = jnp.zeros((rows, dh), F32)
    m, l, acc = lax.fori_loop(0, nk, body, (m0, l0, a0))
    out = acc / l
    for g in range(groups):
        o_ref[0, :, g * dh:(g + 1) * dh] = out[g * tq:(g + 1) * tq].astype(o_ref.dtype)


def _attention(q, kv, tq=256, tk=256):
    b, n, hd = q.shape
    lk = kv.shape[1]
    groups = ATT_HEADS // ATT_KV_HEADS
    gw = groups * LANES
    tq = min(tq, n)
    tk = min(tk, lk)
    return pl.pallas_call(
        functools.partial(_attn_kernel, groups=groups, tq=tq, tk=tk),
        grid=(b, ATT_KV_HEADS, n // tq),
        in_specs=[pl.BlockSpec((1, tq, gw), lambda bi, h, i: (bi, i, h)),
                  pl.BlockSpec((1, lk, LANES), lambda bi, h, i: (bi, 0, h)),
                  pl.BlockSpec((1, lk, LANES), lambda bi, h, i: (bi, 0, ATT_KV_HEADS + h))],
        out_specs=pl.BlockSpec((1, tq, gw), lambda bi, h, i: (bi, i, h)),
        out_shape=jax.ShapeDtypeStruct((b, n, hd), BF16),
        compiler_params=_cparams("parallel", "parallel", "parallel"),
        name="gqa_attention",
    )(q, kv, kv)


def _plain_prologue(a_ref):
    return a_ref[0]


def _gdn_prologue(of_ref, ob_ref, z_ref, g_ref):
    outs = []
    g = g_ref[...]
    for h in range(GDN_V_HEADS):
        sl = slice(h * LANES, (h + 1) * LANES)
        o = of_ref[0, :, sl] + ob_ref[0, :, sl]
        o = o * lax.rsqrt(jnp.mean(o * o, axis=-1, keepdims=True) + EPS) * g
        outs.append((o * _silu(z_ref[0, :, sl].astype(F32))).astype(BF16))
    return jnp.concatenate(outs, axis=-1)


def _ret_prologue(of_ref, ob_ref, gate_ref):
    outs = []
    for h in range(RET_HEADS):
        sl = slice(h * RET_V_DIM, (h + 1) * RET_V_DIM)
        o = of_ref[0, :, sl] + ob_ref[0, :, sl]
        o = o * lax.rsqrt(jnp.mean(o * o, axis=-1, keepdims=True) + EPS)
        outs.append((o * _silu(gate_ref[0, :, sl].astype(F32))).astype(BF16))
    return jnp.concatenate(outs, axis=-1)


def _out_res_kernel(*refs, prologue, n_in):
    w_ref, x_ref, ga_ref, o_ref = refs[n_in:]
    a = prologue(*refs[:n_in])
    y = jnp.dot(a, w_ref[...], preferred_element_type=F32)
    o_ref[0] = x_ref[0] + ga_ref[0] * y


def _out_res(prologue, ins, in_specs, w, x, ga, tm):
    b, l, d = x.shape
    k = w.shape[0]
    return pl.pallas_call(
        functools.partial(_out_res_kernel, prologue=prologue, n_in=len(ins)),
        grid=(b, l // tm),
        in_specs=in_specs + [pl.BlockSpec((k, d), lambda bi, i: (0, 0)),
                             pl.BlockSpec((1, tm, d), lambda bi, i: (bi, i, 0)),
                             pl.BlockSpec((1, 1, d), lambda bi, i: (bi, 0, 0))],
        out_specs=pl.BlockSpec((1, tm, d), lambda bi, i: (bi, i, 0)),
        out_shape=jax.ShapeDtypeStruct((b, l, d), F32),
        compiler_params=_cparams("parallel", "parallel"),
        name="out_proj_residual",
    )(*ins, w, x, ga)


def _row_spec(tm, width, col=0):
    return pl.BlockSpec((1, tm, width), lambda bi, i: (bi, i, col))


def _shift_rows(x, halo_row, shift):
    tm = x.shape[0]
    rolled = pltpu.roll(x, shift % tm, 0)
    r = lax.broadcasted_iota(jnp.int32, x.shape, 0)
    edge = (r < shift) if shift > 0 else (r >= tm + shift)
    return jnp.where(edge, halo_row, rolled)


def _ffn_out_kernel(val_ref, gate_ref, prev_ref, next_ref, cw_ref, cb_ref, w_ref, x_ref, ga_ref,
                    o_ref, acc_ref):
    i, kk = pl.program_id(1), pl.program_id(2)
    g = gate_ref[0].astype(F32)
    hr = BF16_ROWS
    prev = jnp.where(i > 0, prev_ref[0].astype(F32)[hr - 1:hr], 0.0)
    nxt = jnp.where(i < pl.num_programs(1) - 1, next_ref[0].astype(F32)[0:1], 0.0)
    cw = cw_ref[...]
    conv = (cw[0:1] * _shift_rows(g, prev, 1) + cw[1:2] * g
            + cw[2:3] * _shift_rows(g, nxt, -1) + cb_ref[...])
    gelu = 0.5 * conv * (1.0 + lax.erf(conv * (2.0 ** -0.5)))
    act = (gelu * val_ref[0].astype(F32)).astype(BF16)
    y = jnp.dot(act, w_ref[...], preferred_element_type=F32)

    @pl.when(kk == 0)
    def _():
        acc_ref[...] = y

    @pl.when(kk > 0)
    def _():
        acc_ref[...] += y

    @pl.when(kk == pl.num_programs(2) - 1)
    def _():
        o_ref[0] = x_ref[0] + ga_ref[0] * acc_ref[...]


def _ffn_out(u, conv_w, conv_b, w_out, x, ga, tm=512):
    b, l, d = x.shape
    f = w_out.shape[0]
    tm = min(tm, l)
    tk = f // 2
    nkb = f // tk
    hb = tm // BF16_ROWS
    last_hb = l // BF16_ROWS - 1
    return pl.pallas_call(
        _ffn_out_kernel,
        grid=(b, l // tm, nkb),
        in_specs=[pl.BlockSpec((1, tm, tk), lambda bi, i, k: (bi, i, k)),
                  pl.BlockSpec((1, tm, tk), lambda bi, i, k: (bi, i, nkb + k)),
                  pl.BlockSpec((1, BF16_ROWS, tk),
                               lambda bi, i, k: (bi, jnp.maximum(i * hb - 1, 0), nkb + k)),
                  pl.BlockSpec((1, BF16_ROWS, tk),
                               lambda bi, i, k: (bi, jnp.minimum((i + 1) * hb, last_hb), nkb + k)),
                  pl.BlockSpec((FFN_CONV_W, tk), lambda bi, i, k: (0, k)),
                  pl.BlockSpec((1, tk), lambda bi, i, k: (0, k)),
                  pl.BlockSpec((tk, d), lambda bi, i, k: (k, 0)),
                  pl.BlockSpec((1, tm, d), lambda bi, i, k: (bi, i, 0)),
                  pl.BlockSpec((1, 1, d), lambda bi, i, k: (bi, 0, 0))],
        out_specs=pl.BlockSpec((1, tm, d), lambda bi, i, k: (bi, i, 0)),
        out_shape=jax.ShapeDtypeStruct((b, l, d), F32),
        scratch_shapes=[pltpu.VMEM((tm, d), F32)],
        compiler_params=_cparams("parallel", "parallel", "arbitrary"),
        name="convglu_out",
    )(u, u, u, u, conv_w, conv_b.reshape(1, f), w_out, x, ga)


def _gdn_conv_kernel(x_ref, prev_ref, next_ref, cw_ref, o_ref, *, n_norm):
    i, c = pl.program_id(1), pl.program_id(2)
    x = x_ref[0].astype(F32)
    hr = BF16_ROWS
    first, last = i == 0, i == pl.num_programs(1) - 1
    p = jnp.where(first, 0.0, prev_ref[0].astype(F32))
    n = jnp.where(last, 0.0, next_ref[0].astype(F32))
    cw = cw_ref[...]
    tm = x.shape[0]
    r = lax.broadcasted_iota(jnp.int32, x.shape, 0)
    xm2 = jnp.where(r == 0, p[hr - 2:hr - 1], jnp.where(r == 1, p[hr - 1:hr], pltpu.roll(x, 2, 0)))
    xm1 = jnp.where(r == 0, p[hr - 1:hr], pltpu.roll(x, 1, 0))
    xp1 = jnp.where(r == tm - 1, n[0:1], pltpu.roll(x, tm - 1, 0))
    xp2 = jnp.where(r == tm - 1, n[1:2], jnp.where(r == tm - 2, n[0:1], pltpu.roll(x, tm - 2, 0)))
    y = cw[0:1] * xm2 + cw[1:2] * xm1 + cw[2:3] * x + cw[3:4] * xp1 + cw[4:5] * xp2
    y = _silu(y)

    @pl.when(c < n_norm)
    def _():
        o_ref[0] = (y * lax.rsqrt(jnp.sum(y * y, axis=-1, keepdims=True) + EPS)).astype(o_ref.dtype)

    @pl.when(c >= n_norm)
    def _():
        o_ref[0] = y.astype(o_ref.dtype)


def _gdn_conv(pre, conv_w, n_norm, tm=1024):
    b, l, _ = pre.shape
    ch = conv_w.shape[1]
    tm = min(tm, l)
    hb = tm // BF16_ROWS
    last_hb = l // BF16_ROWS - 1
    return pl.pallas_call(
        functools.partial(_gdn_conv_kernel, n_norm=n_norm),
        grid=(b, l // tm, ch // LANES),
        in_specs=[pl.BlockSpec((1, tm, LANES), lambda bi, i, c: (bi, i, c)),
                  pl.BlockSpec((1, BF16_ROWS, LANES),
                               lambda bi, i, c: (bi, jnp.maximum(i * hb - 1, 0), c)),
                  pl.BlockSpec((1, BF16_ROWS, LANES),
                               lambda bi, i, c: (bi, jnp.minimum((i + 1) * hb, last_hb), c)),
                  pl.BlockSpec((GDN_CONV_W, LANES), lambda bi, i, c: (0, c))],
        out_specs=pl.BlockSpec((1, tm, LANES), lambda bi, i, c: (bi, i, c)),
        out_shape=jax.ShapeDtypeStruct((b, l, ch), BF16),
        compiler_params=_cparams("parallel", "parallel", "parallel"),
        name="gdn_conv",
    )(pre, pre, pre, conv_w)


def _unit_tri_inverse_minus_eye(a, rr, cc):
    c = a.shape[0]
    blk = GDN_INV_BLOCK
    d = jnp.where((rr // blk) == (cc // blk), a, 0.0)
    lo = a - d

    def neumann(m, order):
        m2 = _dot(m, m)
        r = m2 - m - _dot(m, m2)
        p, mp = 4, m2
        while p < order:
            mp = _dot(mp, mp)
            r = r + mp + _dot(r, mp)
            p *= 2
        return r

    r0 = neumann(d, blk)
    n = lo + _dot(r0, lo)
    rn = neumann(n, c // blk)
    return r0 + rn + _dot(rn, r0)


def _split3(x):
    x1 = x.astype(BF16)
    r1 = x - x1.astype(F32)
    x2 = r1.astype(BF16)
    x3 = (r1 - x2.astype(F32)).astype(BF16)
    return x1, x2, x3


def _gdn_chain(q, k, v, kk, qk, gcol, bcol, s, reverse):
    c = q.shape[0]
    dk = q.shape[1]
    scale = dk ** -0.5
    rr = lax.broadcasted_iota(jnp.int32, (c, c), 0)
    cc = lax.broadcasted_iota(jnp.int32, (c, c), 1)
    incl = (rr <= cc) if reverse else (rr >= cc)
    strict = (rr < cc) if reverse else (rr > cc)
    sel = jnp.concatenate([incl.astype(BF16), jnp.ones((c, c), BF16)], axis=0)
    gb = jnp.broadcast_to(gcol, (c, LANES))
    g3 = jnp.concatenate(_split3(gb), axis=1)
    cs = jnp.dot(sel, g3, preferred_element_type=F32)
    cs = cs[:, :LANES] + cs[:, LANES:2 * LANES] + cs[:, 2 * LANES:]
    gc, gtot = cs[:c], cs[c:]
    decay = jnp.where(incl, jnp.exp(jnp.minimum(gc - gc.T, 0.0)), 0.0)
    a = jnp.where(strict, bcol * kk * decay, 0.0)
    tm = _unit_tri_inverse_minus_eye(a, rr, cc)
    kf = k.astype(F32)
    egc = jnp.exp(gc)
    x = jnp.concatenate([v.astype(F32) * bcol, kf * (bcol * egc)], axis=1)
    uw = x + _dot(tm, x)
    u, w = uw[:, :dk], uw[:, dk:]
    qkm = qk * decay * scale
    qd = q.astype(F32) * (egc * scale)
    kd = kf * jnp.exp(gtot - gc)
    gl = jnp.exp(gtot[0:1, :])
    sb = s.astype(BF16)
    ws = jnp.dot(jnp.concatenate([w, qd], axis=0).astype(BF16), sb, preferred_element_type=F32)
    v_new = u - ws[:c]
    vnb = v_new.astype(BF16)
    o = ws[c:] + jnp.dot(qkm.astype(BF16), vnb, preferred_element_type=F32)
    s_new = s * gl + jnp.dot(kd.T.astype(BF16), vnb, preferred_element_type=F32)
    return o, s_new


def _gdn_core_kernel(qf_ref, kf_ref, vf_ref, abf_ref, qb_ref, kb_ref, vb_ref, abb_ref,
                     alog_ref, dtb_ref, s0_ref, of_ref, ob_ref, sout_ref, s_ref):
    t = pl.program_id(2)
    h = pl.program_id(1)
    rep = GDN_V_HEADS // GDN_QK_HEADS

    @pl.when(t == 0)
    def _():
        for e in range(rep):
            for d in range(2):
                s_ref[e * 2 + d] = s0_ref[0, e, d]

    lane = lax.broadcasted_iota(jnp.int32, (GDN_CHUNK, LANES), 1)
    for d, (q_ref, k_ref, v_ref, ab_ref, o_ref) in enumerate(
            ((qf_ref, kf_ref, vf_ref, abf_ref, of_ref), (qb_ref, kb_ref, vb_ref, abb_ref, ob_ref))):
        q, k = q_ref[0], k_ref[0]
        kk = _dot_nt(k, k)
        qk = _dot_nt(q, k)
        ab = ab_ref[0]
        sp = ab + dtb_ref[...]
        sp = jnp.maximum(sp, 0.0) + jnp.log1p(jnp.exp(-jnp.abs(sp)))
        gall = -jnp.exp(alog_ref[...]) * sp
        ball = _sigmoid(ab)
        for e in range(rep):
            col = d * GDN_V_HEADS + h * rep + e
            gcol = jnp.sum(jnp.where(lane == col, gall, 0.0), axis=-1, keepdims=True)
            bcol = jnp.sum(jnp.where(lane == col + 2 * GDN_V_HEADS, ball, 0.0), axis=-1,
                           keepdims=True)
            v = v_ref[0, :, e * LANES:(e + 1) * LANES]
            o, s_new = _gdn_chain(q, k, v, kk, qk, gcol, bcol, s_ref[e * 2 + d], reverse=d == 1)
            o_ref[0, :, e * LANES:(e + 1) * LANES] = o
            s_ref[e * 2 + d] = s_new

    @pl.when(t == pl.num_programs(2) - 1)
    def _():
        for e in range(rep):
            for d in range(2):
                sout_ref[0, e, d] = s_ref[e * 2 + d]


def _gdn_core(qkv, ab, alog_row, dtb_row, s0):
    b, l, _ = qkv.shape
    c = GDN_CHUNK
    nt = l // c
    hq, hv = GDN_QK_HEADS, GDN_V_HEADS
    rep = hv // hq
    fwd = lambda off: (lambda bi, h, t: (bi, t, off + h))
    bwd = lambda off: (lambda bi, h, t: (bi, nt - 1 - t, off + h))
    vw = rep * LANES

    def specs(mk, ab_map):
        return [pl.BlockSpec((1, c, LANES), mk(0)),
                pl.BlockSpec((1, c, LANES), mk(hq)),
                pl.BlockSpec((1, c, vw), mk(2 * hq // rep)),
                pl.BlockSpec((1, c, LANES), ab_map)]

    ab_fwd = lambda bi, h, t: (bi, t, 0)
    ab_bwd = lambda bi, h, t: (bi, nt - 1 - t, 0)

    row = pl.BlockSpec((1, LANES), lambda bi, h, t: (0, 0))
    st = pl.BlockSpec((1, rep, 2, LANES, LANES), lambda bi, h, t: (bi, h, 0, 0, 0))
    return pl.pallas_call(
        _gdn_core_kernel,
        grid=(b, hq, nt),
        in_specs=specs(fwd, ab_fwd) + specs(bwd, ab_bwd) + [row, row, st],
        out_specs=[pl.BlockSpec((1, c, vw), fwd(0)), pl.BlockSpec((1, c, vw), bwd(0)), st],
        out_shape=[jax.ShapeDtypeStruct((b, l, hv * LANES), F32),
                   jax.ShapeDtypeStruct((b, l, hv * LANES), F32),
                   jax.ShapeDtypeStruct((b, hv, 2, LANES, LANES), F32)],
        scratch_shapes=[pltpu.VMEM((2 * rep, LANES, LANES), F32)],
        compiler_params=_cparams("parallel", "parallel", "arbitrary"),
        name="gdn_core",
    )(qkv, qkv, qkv, ab, qkv, qkv, qkv, ab, alog_row, dtb_row, s0)


def _ret_core_kernel(qf_ref, kf_ref, vf_ref, qb_ref, kb_ref, vb_ref, lg_ref, s0_ref,
                     of_ref, ob_ref, sout_ref, s_ref, mask_ref):
    t = pl.program_id(2)
    c = qf_ref.shape[1]
    lgf, lgb = lg_ref[0, 0:1, :], lg_ref[0, 1:2, :]

    @pl.when(t == 0)
    def _():
        s_ref[0] = s0_ref[0, 0, 0]
        s_ref[1] = s0_ref[0, 0, 1]
        rr = lax.broadcasted_iota(jnp.int32, (c, c), 0)
        cc = lax.broadcasted_iota(jnp.int32, (c, c), 1)
        diff = (rr - cc).astype(F32)
        mask_ref[...] = jnp.where(diff > 0, jnp.exp(diff * lgf[:, :c]),
                                  jnp.where(diff < 0, jnp.exp(-diff * lgb[:, :c]), 2.0))

    pos = lax.broadcasted_iota(jnp.int32, (c, LANES), 0).astype(F32)
    lgf1, lgb1 = lgf[:, :LANES], lgb[:, :LANES]
    q, k, v = qf_ref[0], kf_ref[0], vf_ref[0]
    qf32, kf32 = q.astype(F32), k.astype(F32)
    sc = _dot_nt(q, k) * mask_ref[...]
    sf = s_ref[0]
    of_ref[0] = _dot(sc, v) + _dot(qf32 * jnp.exp((pos + 1.0) * lgf1), sf)
    kd = kf32 * jnp.exp((c - 1.0 - pos) * lgf1)
    s_ref[0] = sf * jnp.exp(c * lgf) + _dot(kd.T, v)
    q, k, v = qb_ref[0], kb_ref[0], vb_ref[0]
    sb = s_ref[1]
    ob_ref[0] = _dot(q.astype(F32) * jnp.exp((c - pos) * lgb1), sb)
    kd = k.astype(F32) * jnp.exp(pos * lgb1)
    s_ref[1] = sb * jnp.exp(c * lgb) + _dot(kd.T, v)

    @pl.when(t == pl.num_programs(2) - 1)
    def _():
        sout_ref[0, 0, 0] = s_ref[0]
        sout_ref[0, 0, 1] = s_ref[1]


def _ret_core(qk, pre, lg, s0):
    b, l, _ = qk.shape
    c = min(RET_CHUNK, l)
    nt = l // c
    h_, dk, dv = RET_HEADS, RET_QK_DIM, RET_V_DIM
    fwd = lambda off: (lambda bi, h, t: (bi, t, off + h))
    bwd = lambda off: (lambda bi, h, t: (bi, nt - 1 - t, off + h))
    v_off = 2 * h_ * dk // dv

    def specs(mk):
        return [pl.BlockSpec((1, c, dk), mk(0)), pl.BlockSpec((1, c, dk), mk(h_)),
                pl.BlockSpec((1, c, dv), mk(v_off))]

    st = pl.BlockSpec((1, 1, 2, dk, dv), lambda bi, h, t: (bi, h, 0, 0, 0))
    return pl.pallas_call(
        _ret_core_kernel,
        grid=(b, h_, nt),
        in_specs=specs(fwd) + specs(bwd) + [pl.BlockSpec((1, 2, dv), lambda bi, h, t: (h, 0, 0)), st],
        out_specs=[pl.BlockSpec((1, c, dv), fwd(0)), pl.BlockSpec((1, c, dv), bwd(0)), st],
        out_shape=[jax.ShapeDtypeStruct((b, l, h_ * dv), F32),
                   jax.ShapeDtypeStruct((b, l, h_ * dv), F32),
                   jax.ShapeDtypeStruct((b, h_, 2, dk, dv), F32)],
        scratch_shapes=[pltpu.VMEM((2, dk, dv), F32), pltpu.VMEM((c, c), F32)],
        compiler_params=_cparams("parallel", "parallel", "arbitrary"),
        name="retention_core",
    )(qk, qk, pre, qk, qk, pre, lg, s0)


def _final_norm_kernel(x_ref, g_ref, o_ref):
    x = x_ref[0]
    o_ref[0] = x * lax.rsqrt(jnp.mean(x * x, axis=-1, keepdims=True) + EPS) * g_ref[...]


def _final_norm(x, g, tm=1024):
    b, l, d = x.shape
    tm = min(tm, l)
    return pl.pallas_call(
        _final_norm_kernel,
        grid=(b, l // tm),
        in_specs=[pl.BlockSpec((1, tm, d), lambda bi, i: (bi, i, 0)),
                  pl.BlockSpec((1, d), lambda bi, i: (0, 0))],
        out_specs=pl.BlockSpec((1, tm, d), lambda bi, i: (bi, i, 0)),
        out_shape=jax.ShapeDtypeStruct((b, l, d), F32),
        compiler_params=_cparams("parallel", "parallel"),
        name="final_norm",
    )(x, g.reshape(1, d))


def _attention_layer(x, xc, m, mc, norm_g, w_in, q_g, k_g, w_out, tables, want_ctx):
    hd = ATT_HEADS * ATT_HEAD_DIM
    kvd = ATT_KV_HEADS * ATT_HEAD_DIM
    w_in = w_in.astype(BF16)
    w_out = w_out.astype(BF16)
    gains = jnp.concatenate([jnp.tile(q_g * ATT_HEAD_DIM ** -0.5, (ATT_HEADS, 1)),
                             jnp.tile(k_g, (ATT_KV_HEADS, 1))], axis=0)
    n_heads = ATT_HEADS + ATT_KV_HEADS

    def project(t, mm, tabs):
        qkv = _proj(t, norm_g, mm[1], mm[0], w_in, BF16, tm=1024, tn=hd + 2 * kvd)
        return _head_prep(qkv, gains, tabs, norm=True), qkv[:, :, hd + kvd:]

    qk, v = project(x, m, tables)
    qkc, vc = project(xc, mc, None)
    kv_c = jnp.concatenate([qkc[:, :, hd:], vc], axis=-1)
    kv_all = jnp.concatenate([jnp.concatenate([qk[:, :, hd:], v], axis=-1), kv_c], axis=1)
    o = _attention(qk, kv_all)
    tm = min(512, x.shape[1])
    x = _out_res(_plain_prologue, [o], [_row_spec(tm, hd)], w_out, x, m[2], tm)
    if want_ctx:
        oc = _attention(qkc, kv_c)
        tmc = min(512, xc.shape[1])
        xc = _out_res(_plain_prologue, [oc], [_row_spec(tmc, hd)], w_out, xc, mc[2], tmc)
    return x, xc


def _gdn_layer(x, xc, m, mc, norm_g, w_in, conv_w, a_log, dt_bias, gn_g, w_out, want_ctx):
    hq, hv, dh = GDN_QK_HEADS, GDN_V_HEADS, GDN_HEAD_DIM
    cw = (2 * hq + hv) * dh
    zw = hv * dh
    w_main = w_in[:, :cw + zw].astype(BF16)
    w_ab = jnp.pad(w_in[:, cw + zw:], ((0, 0), (0, LANES - 4 * hv))).astype(BF16)
    w_out = w_out.astype(BF16)
    pad = (0, LANES - 2 * hv)
    alog_row = jnp.pad(a_log.reshape(-1), pad).reshape(1, LANES)
    dtb_row = jnp.pad(dt_bias.reshape(-1), pad).reshape(1, LANES)

    def project(t, mm):
        pre = _proj(t, norm_g, mm[1], mm[0], w_main, BF16, tm=1024, tn=2048)
        ab = _proj(t, norm_g, mm[1], mm[0], w_ab, F32, tm=1024, tn=LANES)
        return _gdn_conv(pre, conv_w, n_norm=2 * hq), pre, ab

    b = x.shape[0]
    qkv_c, pre_c, ab_c = project(xc, mc)
    qkv, pre, ab = project(x, m)
    s0 = jnp.zeros((b, hv, 2, dh, dh), F32)
    ocf, ocb, sc = _gdn_core(qkv_c, ab_c, alog_row, dtb_row, s0)
    of, ob, _ = _gdn_core(qkv, ab, alog_row, dtb_row, sc)

    def out(of_, ob_, pre_, t, mm):
        tm = min(512, t.shape[1])
        zspec = _row_spec(tm, zw, col=cw // zw)
        gspec = pl.BlockSpec((1, LANES), lambda bi, i: (0, 0))
        return _out_res(_gdn_prologue, [of_, ob_, pre_, gn_g.reshape(1, dh)],
                        [_row_spec(tm, zw), _row_spec(tm, zw), zspec, gspec], w_out, t, mm[2], tm)

    x = out(of, ob, pre, x, m)
    if want_ctx:
        xc = out(ocf, ocb, pre_c, xc, mc)
    return x, xc


def _retention_layer(x, xc, m, mc, norm_g, w_in, decay, w_out, tables, want_ctx):
    h_, dk, dv = RET_HEADS, RET_QK_DIM, RET_V_DIM
    qkw = 2 * h_ * dk
    w_in = w_in.astype(BF16)
    w_out = w_out.astype(BF16)
    gains = jnp.concatenate([jnp.full((h_, dk), dk ** -0.5, F32), jnp.ones((h_, dk), F32)], axis=0)
    lg = jnp.broadcast_to(-decay.astype(F32).T[:, :, None], (h_, 2, dv))

    def project(t, mm, tabs):
        pre = _proj(t, norm_g, mm[1], mm[0], w_in, BF16, tm=1024, tn=2048)
        return _head_prep(pre, gains, tabs, norm=False), pre

    b = x.shape[0]
    qk_c, vg_c = project(xc, mc, None)
    qk, vg = project(x, m, tables)
    s0 = jnp.zeros((b, h_, 2, dk, dv), F32)
    ocf, ocb, sc = _ret_core(qk_c, vg_c, lg, s0)
    of, ob, _ = _ret_core(qk, vg, lg, sc)
    vw = h_ * dv

    def out(of_, ob_, vg_, t, mm):
        tm = min(512, t.shape[1])
        return _out_res(_ret_prologue, [of_, ob_, vg_],
                        [_row_spec(tm, vw), _row_spec(tm, vw), _row_spec(tm, vw, col=2)],
                        w_out, t, mm[2], tm)

    x = out(of, ob, vg, x, m)
    if want_ctx:
        xc = out(ocf, ocb, vg_c, xc, mc)
    return x, xc


def _ffn(x, m, norm_g, w_in, conv_w, conv_b, w_out):
    f = w_out.shape[0]
    u = _proj(x, norm_g, m[4], m[3], w_in.astype(BF16), BF16, tm=1024, tn=f)
    return _ffn_out(u, conv_w, conv_b, w_out.astype(BF16), x, m[5])


def kernel(x, c, ctx, c_ctx, mod_w, mod_b, norm1_g, norm2_g, att_w_in, att_q_g, att_k_g, att_w_out, gdn_w_in, gdn_conv_w, gdn_A_log, gdn_dt_bias, gdn_norm_g, gdn_w_out, ret_w_in, ret_decay, ret_w_out, ffn_w_in, ffn_conv_w, ffn_conv_b, ffn_w_out, final_g):
    b, n, d = x.shape
    depth = mod_w.shape[0]
    cc = jnp.concatenate([c, c_ctx[None, :], jnp.zeros((8 - b - 1, d), F32)], axis=0)
    mods = _modulation(cc, mod_w, mod_b)
    tab_att = _rope_tables(n, ATT_HEAD_DIM)
    tab_ret = _rope_tables(n, RET_QK_DIM)
    xc = ctx
    for i in range(depth):
        last = i == depth - 1
        mi = mods[i].reshape(8, 6, d)
        m = [mi[:b, s][:, None, :] for s in range(6)]
        mc = [jnp.broadcast_to(mi[b, s][None, None, :], (b, 1, d)) for s in range(6)]
        kind, j = i % N_MIXERS, i // N_MIXERS
        if kind == 0:
            x, xc = _attention_layer(x, xc, m, mc, norm1_g[i], att_w_in[j], att_q_g[j], att_k_g[j],
                                     att_w_out[j], tab_att, not last)
        elif kind == 1:
            x, xc = _gdn_layer(x, xc, m, mc, norm1_g[i], gdn_w_in[j], gdn_conv_w[j], gdn_A_log[j],
                               gdn_dt_bias[j], gdn_norm_g[j], gdn_w_out[j], not last)
        else:
            x, xc = _retention_layer(x, xc, m, mc, norm1_g[i], ret_w_in[j], ret_decay[j],
                                     ret_w_out[j], tab_ret, not last)
        x = _ffn(x, m, norm2_g[i], ffn_w_in[i], ffn_conv_w[i], ffn_conv_b[i], ffn_w_out[i])
        if not last:
            xc = _ffn(xc, mc, norm2_g[i], ffn_w_in[i], ffn_conv_w[i], ffn_conv_b[i], ffn_w_out[i])
    return _final_norm(x, final_g)
```

```python
import functools

import jax
import jax.numpy as jnp
from jax import lax
from jax.experimental import pallas as pl
from jax.experimental.pallas import tpu as pltpu

F32 = jnp.float32
BF16 = jnp.bfloat16

EPS = 1e-6
LOG2E = 1.4426950408889634
ROPE_THETA = 10000.0
GRID_W = 64
N_MIXERS = 3

ATT_HEADS, ATT_KV_HEADS, ATT_HEAD_DIM = 8, 2, 128
GDN_QK_HEADS, GDN_V_HEADS, GDN_HEAD_DIM, GDN_CONV_W = 8, 16, 128, 5
RET_HEADS, RET_QK_DIM, RET_V_DIM = 8, 128, 256
FFN_CONV_W = 3

LANES = 128
BF16_ROWS = 16
GDN_CHUNK = 128
GDN_INV_BLOCK = 16
GDN_HEADS_PER_STEP = 4
RET_CHUNK = 256
ATT_MAX_UNROLL = 11
VMEM_LIMIT = 56 * 1024 * 1024


def _cparams(*sem):
    return pltpu.CompilerParams(dimension_semantics=sem, vmem_limit_bytes=VMEM_LIMIT)


def _sigmoid(x):
    return 1.0 / (1.0 + jnp.exp(-x))


def _silu(x):
    return x * _sigmoid(x)


def _dot(a, b):
    return jnp.dot(a.astype(BF16), b.astype(BF16), preferred_element_type=F32)


def _dot_nt(a, b):
    return lax.dot_general(a.astype(BF16), b.astype(BF16), (((1,), (1,)), ((), ())),
                           preferred_element_type=F32)


def _mod_kernel(cc_ref, w_ref, b_ref, o_ref):
    s = _silu(cc_ref[...])
    o_ref[0] = _dot(s, w_ref[0]) + b_ref[0]


def _modulation(cc, mod_w, mod_b):
    depth, d, n6 = mod_w.shape
    tn = n6 // 4
    return pl.pallas_call(
        _mod_kernel,
        grid=(depth, n6 // tn),
        in_specs=[pl.BlockSpec((8, d), lambda l, j: (0, 0)),
                  pl.BlockSpec((1, d, tn), lambda l, j: (l, 0, j)),
                  pl.BlockSpec((1, 1, tn), lambda l, j: (l, 0, j))],
        out_specs=pl.BlockSpec((1, 8, tn), lambda l, j: (l, 0, j)),
        out_shape=jax.ShapeDtypeStruct((depth, 8, n6), F32),
        compiler_params=_cparams("parallel", "parallel"),
        name="modulation",
    )(cc, mod_w, mod_b.reshape(depth, 1, n6))


def _norm_mod(x, g, sc, sh):
    ms = jnp.mean(x * x, axis=-1, keepdims=True)
    return (x * lax.rsqrt(ms + EPS)) * g * (1.0 + sc) + sh


def _proj_kernel(x_ref, g_ref, sc_ref, sh_ref, w_ref, o_ref, h_ref):
    @pl.when(pl.program_id(2) == 0)
    def _():
        h_ref[...] = _norm_mod(x_ref[0], g_ref[...], sc_ref[0], sh_ref[0]).astype(BF16)

    o_ref[0] = jnp.dot(h_ref[...], w_ref[...], preferred_element_type=F32).astype(o_ref.dtype)


def _proj(x, g, sc, sh, w, out_dtype, tm, tn):
    b, l, d = x.shape
    n = w.shape[1]
    tm = min(tm, l)
    tn = min(tn, n)
    return pl.pallas_call(
        _proj_kernel,
        grid=(b, l // tm, n // tn),
        in_specs=[pl.BlockSpec((1, tm, d), lambda bi, i, j: (bi, i, 0)),
                  pl.BlockSpec((1, d), lambda bi, i, j: (0, 0)),
                  pl.BlockSpec((1, 1, d), lambda bi, i, j: (bi, 0, 0)),
                  pl.BlockSpec((1, 1, d), lambda bi, i, j: (bi, 0, 0)),
                  pl.BlockSpec((d, tn), lambda bi, i, j: (0, j))],
        out_specs=pl.BlockSpec((1, tm, tn), lambda bi, i, j: (bi, i, j)),
        out_shape=jax.ShapeDtypeStruct((b, l, n), out_dtype),
        scratch_shapes=[pltpu.VMEM((tm, d), BF16)],
        compiler_params=_cparams("parallel", "parallel", "arbitrary"),
        name="norm_mod_proj",
    )(x, g.reshape(1, d), sc, sh, w)


def _rope_tables(n, head_dim):
    rows = n // GRID_W
    row = jnp.repeat(jnp.arange(rows, dtype=F32), GRID_W)
    col = jnp.tile(jnp.arange(GRID_W, dtype=F32), rows)
    axis_dim = head_dim // 2
    inv_freq = ROPE_THETA ** (-jnp.arange(0, axis_dim, 2, dtype=F32) / axis_dim)
    ar, ac = row[:, None] * inv_freq, col[:, None] * inv_freq
    cr, sr, cc, sc = jnp.cos(ar), jnp.sin(ar), jnp.cos(ac), jnp.sin(ac)
    z = jnp.zeros_like(sr)
    return (jnp.concatenate([cr, cr, cc, cc], axis=-1),
            jnp.concatenate([-sr, z, -sc, z], axis=-1),
            jnp.concatenate([z, sr, z, sc], axis=-1))


def _head_prep_kernel(*refs, norm, rope):
    if rope:
        x_ref, g_ref, c_ref, sa_ref, sb_ref, o_ref = refs
    else:
        x_ref, g_ref, o_ref = refs
    x = x_ref[0].astype(F32)
    if norm:
        x = x * lax.rsqrt(jnp.mean(x * x, axis=-1, keepdims=True) + EPS)
    x = x * g_ref[0]
    if rope:
        x = (x * c_ref[...] + pltpu.roll(x, 96, 1) * sa_ref[...]
             + pltpu.roll(x, 32, 1) * sb_ref[...])
    o_ref[0] = x.astype(o_ref.dtype)


def _head_prep(x, gains, tables, norm, tm=1024):
    b, l, _ = x.shape
    nb = gains.shape[0]
    tm = min(tm, l)
    in_specs = [pl.BlockSpec((1, tm, LANES), lambda bi, i, c: (bi, i, c)),
                pl.BlockSpec((1, 1, LANES), lambda bi, i, c: (c, 0, 0))]
    args = [x, gains.reshape(nb, 1, LANES)]
    if tables is not None:
        in_specs += [pl.BlockSpec((tm, LANES), lambda bi, i, c: (i, 0))] * 3
        args += list(tables)
    return pl.pallas_call(
        functools.partial(_head_prep_kernel, norm=norm, rope=tables is not None),
        grid=(b, l // tm, nb),
        in_specs=in_specs,
        out_specs=pl.BlockSpec((1, tm, LANES), lambda bi, i, c: (bi, i, c)),
        out_shape=jax.ShapeDtypeStruct((b, l, nb * LANES), BF16),
        compiler_params=_cparams("parallel", "parallel", "parallel"),
        name="head_prep",
    )(*args)


def _attn_kernel(q_ref, k_ref, vt_ref, o_ref, *, groups, tq, unroll):
    dh = LANES
    qt = jnp.concatenate([q_ref[0, :, g * dh:(g + 1) * dh].astype(F32).T for g in range(groups)],
                         axis=1).astype(BF16)
    rows = groups * tq
    nk, vrows, tk = vt_ref.shape[2:]

    def scores(j):
        off = pl.multiple_of(j * tk, tk)
        return jnp.dot(k_ref[0, pl.ds(off, tk), :], qt, preferred_element_type=F32)

    def body(it, carry):
        m, acc = carry
        base = it * unroll
        st = scores(base)
        for u in range(unroll):
            st_next = scores(base + u + 1) if u + 1 < unroll else None
            m_new = jnp.maximum(m, jnp.max(st, axis=0, keepdims=True))
            p = jnp.exp2(st - m_new).astype(BF16)
            acc = jnp.exp2(m - m_new) * acc + jnp.dot(vt_ref[0, 0, base + u], p,
                                                     preferred_element_type=F32)
            m, st = m_new, st_next
        return m, acc

    init = (jnp.full((1, rows), -1e30, F32), jnp.zeros((vrows, rows), F32))
    _, acc = lax.fori_loop(0, nk // unroll, body, init)
    out = acc[:dh] / acc[dh:dh + 1]
    for g in range(groups):
        o_ref[0, :, g * dh:(g + 1) * dh] = out[:, g * tq:(g + 1) * tq].T.astype(o_ref.dtype)


def _attention(q, kv, tq=256, tk=256):
    b, n, hd = q.shape
    lk = kv.shape[1]
    kvh = ATT_KV_HEADS
    groups = ATT_HEADS // kvh
    gw = groups * LANES
    tq = min(tq, n)
    tk = min(tk, lk)
    nk = lk // tk
    vt = kv[:, :, kvh * LANES:].reshape(b, nk, tk, kvh, LANES).transpose(0, 3, 1, 4, 2)
    vt = jnp.concatenate([vt, jnp.ones((b, kvh, nk, BF16_ROWS, tk), BF16)], axis=3)
    vrows = LANES + BF16_ROWS
    unroll = max(u for u in range(1, ATT_MAX_UNROLL + 1) if nk % u == 0)
    return pl.pallas_call(
        functools.partial(_attn_kernel, groups=groups, tq=tq, unroll=unroll),
        grid=(b, kvh, n // tq),
        in_specs=[pl.BlockSpec((1, tq, gw), lambda bi, h, i: (bi, i, h)),
                  pl.BlockSpec((1, lk, LANES), lambda bi, h, i: (bi, 0, h)),
                  pl.BlockSpec((1, 1, nk, vrows, tk), lambda bi, h, i: (bi, h, 0, 0, 0))],
        out_specs=pl.BlockSpec((1, tq, gw), lambda bi, h, i: (bi, i, h)),
        out_shape=jax.ShapeDtypeStruct((b, n, hd), BF16),
        compiler_params=_cparams("parallel", "parallel", "parallel"),
        name="gqa_attention",
    )(q, kv, vt)


def _plain_prologue(a_ref):
    return a_ref[0]


def _gdn_prologue(of_ref, ob_ref, z_ref, g_ref):
    outs = []
    g = g_ref[...]
    for h in range(GDN_V_HEADS):
        sl = slice(h * LANES, (h + 1) * LANES)
        o = of_ref[0, :, sl] + ob_ref[0, :, sl]
        o = o * lax.rsqrt(jnp.mean(o * o, axis=-1, keepdims=True) + EPS) * g
        outs.append((o * _silu(z_ref[0, :, sl].astype(F32))).astype(BF16))
    return jnp.concatenate(outs, axis=-1)


def _ret_prologue(of_ref, ob_ref, gate_ref):
    outs = []
    for h in range(RET_HEADS):
        sl = slice(h * RET_V_DIM, (h + 1) * RET_V_DIM)
        o = of_ref[0, :, sl] + ob_ref[0, :, sl]
        o = o * lax.rsqrt(jnp.mean(o * o, axis=-1, keepdims=True) + EPS)
        outs.append((o * _silu(gate_ref[0, :, sl].astype(F32))).astype(BF16))
    return jnp.concatenate(outs, axis=-1)


def _out_res_kernel(*refs, prologue, n_in):
    w_ref, x_ref, ga_ref, o_ref = refs[n_in:]
    a = prologue(*refs[:n_in])
    y = jnp.dot(a, w_ref[...], preferred_element_type=F32)
    o_ref[0] = x_ref[0] + ga_ref[0] * y


def _out_res(prologue, ins, in_specs, w, x, ga, tm):
    b, l, d = x.shape
    k = w.shape[0]
    return pl.pallas_call(
        functools.partial(_out_res_kernel, prologue=prologue, n_in=len(ins)),
        grid=(b, l // tm),
        in_specs=in_specs + [pl.BlockSpec((k, d), lambda bi, i: (0, 0)),
                             pl.BlockSpec((1, tm, d), lambda bi, i: (bi, i, 0)),
                             pl.BlockSpec((1, 1, d), lambda bi, i: (bi, 0, 0))],
        out_specs=pl.BlockSpec((1, tm, d), lambda bi, i: (bi, i, 0)),
        out_shape=jax.ShapeDtypeStruct((b, l, d), F32),
        compiler_params=_cparams("parallel", "parallel"),
        name="out_proj_residual",
    )(*ins, w, x, ga)


def _row_spec(tm, width, col=0):
    return pl.BlockSpec((1, tm, width), lambda bi, i: (bi, i, col))


def _shift_rows(x, halo_row, shift):
    tm = x.shape[0]
    rolled = pltpu.roll(x, shift % tm, 0)
    r = lax.broadcasted_iota(jnp.int32, x.shape, 0)
    edge = (r < shift) if shift > 0 else (r >= tm + shift)
    return jnp.where(edge, halo_row, rolled)


def _ffn_out_kernel(val_ref, gate_ref, prev_ref, next_ref, cw_ref, cb_ref, w_ref, x_ref, ga_ref,
                    o_ref, acc_ref):
    i, kk = pl.program_id(1), pl.program_id(2)
    g = gate_ref[0].astype(F32)
    hr = BF16_ROWS
    prev = jnp.where(i > 0, prev_ref[0].astype(F32)[hr - 1:hr], 0.0)
    nxt = jnp.where(i < pl.num_programs(1) - 1, next_ref[0].astype(F32)[0:1], 0.0)
    cw = cw_ref[...]
    conv = (cw[0:1] * _shift_rows(g, prev, 1) + cw[1:2] * g
            + cw[2:3] * _shift_rows(g, nxt, -1) + cb_ref[...])
    gelu = 0.5 * conv * (1.0 + lax.erf(conv * (2.0 ** -0.5)))
    act = (gelu * val_ref[0].astype(F32)).astype(BF16)
    y = jnp.dot(act, w_ref[...], preferred_element_type=F32)

    @pl.when(kk == 0)
    def _():
        acc_ref[...] = y

    @pl.when(kk > 0)
    def _():
        acc_ref[...] += y

    @pl.when(kk == pl.num_programs(2) - 1)
    def _():
        o_ref[0] = x_ref[0] + ga_ref[0] * acc_ref[...]


def _ffn_out(u, conv_w, conv_b, w_out, x, ga, tm=512):
    b, l, d = x.shape
    f = w_out.shape[0]
    tm = min(tm, l)
    tk = f // 2
    nkb = f // tk
    hb = tm // BF16_ROWS
    last_hb = l // BF16_ROWS - 1
    return pl.pallas_call(
        _ffn_out_kernel,
        grid=(b, l // tm, nkb),
        in_specs=[pl.BlockSpec((1, tm, tk), lambda bi, i, k: (bi, i, k)),
                  pl.BlockSpec((1, tm, tk), lambda bi, i, k: (bi, i, nkb + k)),
                  pl.BlockSpec((1, BF16_ROWS, tk),
                               lambda bi, i, k: (bi, jnp.maximum(i * hb - 1, 0), nkb + k)),
                  pl.BlockSpec((1, BF16_ROWS, tk),
                               lambda bi, i, k: (bi, jnp.minimum((i + 1) * hb, last_hb), nkb + k)),
                  pl.BlockSpec((FFN_CONV_W, tk), lambda bi, i, k: (0, k)),
                  pl.BlockSpec((1, tk), lambda bi, i, k: (0, k)),
                  pl.BlockSpec((tk, d), lambda bi, i, k: (k, 0)),
                  pl.BlockSpec((1, tm, d), lambda bi, i, k: (bi, i, 0)),
                  pl.BlockSpec((1, 1, d), lambda bi, i, k: (bi, 0, 0))],
        out_specs=pl.BlockSpec((1, tm, d), lambda bi, i, k: (bi, i, 0)),
        out_shape=jax.ShapeDtypeStruct((b, l, d), F32),
        scratch_shapes=[pltpu.VMEM((tm, d), F32)],
        compiler_params=_cparams("parallel", "parallel", "arbitrary"),
        name="convglu_out",
    )(u, u, u, u, conv_w, conv_b.reshape(1, f), w_out, x, ga)


def _gdn_conv_kernel(x_ref, prev_ref, next_ref, cw_ref, o_ref, *, n_norm):
    i, c = pl.program_id(1), pl.program_id(2)
    x = x_ref[0].astype(F32)
    hr = BF16_ROWS
    first, last = i == 0, i == pl.num_programs(1) - 1
    p = jnp.where(first, 0.0, prev_ref[0].astype(F32))
    n = jnp.where(last, 0.0, next_ref[0].astype(F32))
    cw = cw_ref[...]
    tm = x.shape[0]
    r = lax.broadcasted_iota(jnp.int32, x.shape, 0)
    xm2 = jnp.where(r == 0, p[hr - 2:hr - 1], jnp.where(r == 1, p[hr - 1:hr], pltpu.roll(x, 2, 0)))
    xm1 = jnp.where(r == 0, p[hr - 1:hr], pltpu.roll(x, 1, 0))
    xp1 = jnp.where(r == tm - 1, n[0:1], pltpu.roll(x, tm - 1, 0))
    xp2 = jnp.where(r == tm - 1, n[1:2], jnp.where(r == tm - 2, n[0:1], pltpu.roll(x, tm - 2, 0)))
    y = cw[0:1] * xm2 + cw[1:2] * xm1 + cw[2:3] * x + cw[3:4] * xp1 + cw[4:5] * xp2
    y = _silu(y)

    @pl.when(c < n_norm)
    def _():
        o_ref[0] = (y * lax.rsqrt(jnp.sum(y * y, axis=-1, keepdims=True) + EPS)).astype(o_ref.dtype)

    @pl.when(c >= n_norm)
    def _():
        o_ref[0] = y.astype(o_ref.dtype)


def _gdn_conv(pre, conv_w, n_norm, tm=1024):
    b, l, _ = pre.shape
    ch = conv_w.shape[1]
    tm = min(tm, l)
    hb = tm // BF16_ROWS
    last_hb = l // BF16_ROWS - 1
    return pl.pallas_call(
        functools.partial(_gdn_conv_kernel, n_norm=n_norm),
        grid=(b, l // tm, ch // LANES),
        in_specs=[pl.BlockSpec((1, tm, LANES), lambda bi, i, c: (bi, i, c)),
                  pl.BlockSpec((1, BF16_ROWS, LANES),
                               lambda bi, i, c: (bi, jnp.maximum(i * hb - 1, 0), c)),
                  pl.BlockSpec((1, BF16_ROWS, LANES),
                               lambda bi, i, c: (bi, jnp.minimum((i + 1) * hb, last_hb), c)),
                  pl.BlockSpec((GDN_CONV_W, LANES), lambda bi, i, c: (0, c))],
        out_specs=pl.BlockSpec((1, tm, LANES), lambda bi, i, c: (bi, i, c)),
        out_shape=jax.ShapeDtypeStruct((b, l, ch), BF16),
        compiler_params=_cparams("parallel", "parallel", "parallel"),
        name="gdn_conv",
    )(pre, pre, pre, conv_w)


def _mm(xs, ys):
    return [jnp.dot(x, y, preferred_element_type=F32) for x, y in zip(xs, ys)]


def _bf(xs):
    return [x.astype(BF16) for x in xs]


def _neumann_minus_eye(ms, order):
    mb = _bf(ms)
    m2 = _mm(mb, mb)
    pb = _bf(m2)
    m3 = _mm(mb, pb)
    r = [b - a - c for a, b, c in zip(ms, m2, m3)]
    p, mp = 4, m2
    while p < order:
        mp = _mm(pb, pb)
        pb = _bf(mp)
        rm = _mm(_bf(r), pb)
        r = [x + y + z for x, y, z in zip(r, mp, rm)]
        p *= 2
    return r


def _unit_tri_inverse_minus_eye(a_list, same_block):
    c = a_list[0].shape[0]
    d = [jnp.where(same_block, a, 0.0) for a in a_list]
    lo = [a - x for a, x in zip(a_list, d)]
    r0 = _neumann_minus_eye(d, GDN_INV_BLOCK)
    r0b = _bf(r0)
    n = [x + y for x, y in zip(lo, _mm(r0b, _bf(lo)))]
    rn = _neumann_minus_eye(n, c // GDN_INV_BLOCK)
    return [x + y + z for x, y, z in zip(r0, rn, _mm(_bf(rn), r0b))]


def _gdn_core_kernel(qf_ref, kf_ref, vf_ref, abf_ref, qb_ref, kb_ref, vb_ref, abb_ref,
                     alog_ref, dtb_ref, s0_ref, of_ref, ob_ref, sout_ref, s_ref, *, hb):
    t = pl.program_id(2)
    grp = pl.program_id(1)
    rep = GDN_V_HEADS // GDN_QK_HEADS
    c, dk = GDN_CHUNK, GDN_HEAD_DIM
    scale = dk ** -0.5

    @pl.when(t == 0)
    def _():
        s_ref[...] = s0_ref[0]

    rr = lax.broadcasted_iota(jnp.int32, (c, c), 0)
    cc = lax.broadcasted_iota(jnp.int32, (c, c), 1)
    same_block = (rr // GDN_INV_BLOCK) == (cc // GDN_INV_BLOCK)
    lane = lax.broadcasted_iota(jnp.int32, (c, LANES), 1)
    ng = 2 * GDN_V_HEADS
    dirs = ((qf_ref, kf_ref, vf_ref, abf_ref, of_ref, rr >= cc, rr > cc),
            (qb_ref, kb_ref, vb_ref, abb_ref, ob_ref, rr <= cc, rr < cc))

    gcs, gtots, balls, kks, qks = [], [], [], [], []
    for q_ref, k_ref, _, ab_ref, _, incl, _ in dirs:
        ab = ab_ref[0]
        sp = ab + dtb_ref[...]
        sp = jnp.maximum(sp, 0.0) + jnp.log1p(jnp.exp(-jnp.abs(sp)))
        gall = -jnp.exp(alog_ref[...]) * sp
        balls.append(_sigmoid(ab))
        g1 = gall.astype(BF16)
        r1 = gall - g1.astype(F32)
        g2 = r1.astype(BF16)
        g3 = (r1 - g2.astype(F32)).astype(BF16)
        packed = jnp.where(lane < ng, g1.astype(F32),
                           jnp.where(lane < 2 * ng, pltpu.roll(g2.astype(F32), ng, 1),
                                     jnp.where(lane < 3 * ng,
                                               pltpu.roll(g3.astype(F32), 2 * ng, 1), 0.0)))
        sel = jnp.concatenate([jnp.where(incl, 1.0, 0.0), jnp.ones((c, c), F32)], axis=0)
        cs = _dot(sel, packed)
        cs = cs + pltpu.roll(cs, LANES - ng, 1) + pltpu.roll(cs, LANES - 2 * ng, 1)
        gcs.append(cs[:c])
        gtots.append(cs[c:])
        for hh in range(hb):
            sl = slice(hh * dk, (hh + 1) * dk)
            k = k_ref[0, :, sl]
            g = _dot_nt(jnp.concatenate([k, q_ref[0, :, sl]], axis=0), k)
            kks.append(g[:c])
            qks.append(g[c:])

    chains = [(d, hh, e) for d in range(2) for hh in range(hb) for e in range(rep)]
    a_list, xs, qkms, qds, kdts, gls = [], [], [], [], [], []
    for d, hh, e in chains:
        q_ref, k_ref, v_ref, _, _, incl, strict = dirs[d]
        col = d * GDN_V_HEADS + (grp * hb + hh) * rep + e
        pick = lambda m, j: jnp.sum(jnp.where(lane == j, m, 0.0), axis=-1, keepdims=True)
        gc = jnp.broadcast_to(pick(gcs[d], col), (c, c))
        gtot = pick(gtots[d], col)
        bcol = pick(balls[d], col + ng)
        decay = jnp.where(incl, jnp.exp(jnp.minimum(gc - gc.T, 0.0)), 0.0)
        a_list.append(jnp.where(strict, bcol * kks[d * hb + hh] * decay, 0.0))
        kf = k_ref[0, :, hh * dk:(hh + 1) * dk].astype(F32)
        qf = q_ref[0, :, hh * dk:(hh + 1) * dk].astype(F32)
        vf = v_ref[0, :, (hh * rep + e) * dk:(hh * rep + e + 1) * dk].astype(F32)
        egc = jnp.exp(gc[:, :1])
        xs.append(jnp.concatenate([vf * bcol, kf * (bcol * egc)], axis=1))
        qkms.append((qks[d * hb + hh] * decay * scale).astype(BF16))
        qds.append(qf * (egc * scale))
        kdts.append((kf * jnp.exp(gtot - gc[:, :1])).T.astype(BF16))
        gls.append(jnp.broadcast_to(jnp.exp(gtot[0:1, :]), (1, dk)))

    tms = _unit_tri_inverse_minus_eye(a_list, same_block)
    uws = [x + y for x, y in zip(xs, _mm(_bf(tms), _bf(xs)))]

    ss = [s_ref[i] for i in range(len(chains))]
    sbs = _bf(ss)
    wss = _mm([jnp.concatenate([uw[:, dk:], qd], axis=0).astype(BF16) for uw, qd in zip(uws, qds)],
              sbs)
    vns = [(uw[:, :dk] - ws[:c]).astype(BF16) for uw, ws in zip(uws, wss)]
    ovs = _mm(qkms, vns)
    kvs = _mm(kdts, vns)
    for i, (d, hh, e) in enumerate(chains):
        o_ref = dirs[d][4]
        o_ref[0, :, (hh * rep + e) * dk:(hh * rep + e + 1) * dk] = wss[i][c:] + ovs[i]
        s_ref[i] = ss[i] * gls[i] + kvs[i]

    @pl.when(t == pl.num_programs(2) - 1)
    def _():
        sout_ref[0] = s_ref[...]


def _gdn_core(qkv, ab, alog_row, dtb_row, s0, hb=GDN_HEADS_PER_STEP):
    b, l, _ = qkv.shape
    c = GDN_CHUNK
    nt = l // c
    hq, hv = GDN_QK_HEADS, GDN_V_HEADS
    rep = hv // hq
    ngrp = hq // hb
    fwd = lambda off: (lambda bi, g, t: (bi, t, off + g))
    bwd = lambda off: (lambda bi, g, t: (bi, nt - 1 - t, off + g))
    qw, vw = hb * LANES, hb * rep * LANES

    def specs(mk, ab_map):
        return [pl.BlockSpec((1, c, qw), mk(0)),
                pl.BlockSpec((1, c, qw), mk(ngrp)),
                pl.BlockSpec((1, c, vw), mk(ngrp)),
                pl.BlockSpec((1, c, LANES), ab_map)]

    ab_fwd = lambda bi, g, t: (bi, t, 0)
    ab_bwd = lambda bi, g, t: (bi, nt - 1 - t, 0)
    row = pl.BlockSpec((1, LANES), lambda bi, g, t: (0, 0))
    st = pl.BlockSpec((1, 2 * hb * rep, LANES, LANES), lambda bi, g, t: (bi * ngrp + g, 0, 0, 0))
    s0g = s0.reshape(b, 2, ngrp, hb * rep, LANES, LANES).transpose(0, 2, 1, 3, 4, 5)
    s0g = s0g.reshape(b * ngrp, 2 * hb * rep, LANES, LANES)
    of, ob, sg = pl.pallas_call(
        functools.partial(_gdn_core_kernel, hb=hb),
        grid=(b, ngrp, nt),
        in_specs=specs(fwd, ab_fwd) + specs(bwd, ab_bwd) + [row, row, st],
        out_specs=[pl.BlockSpec((1, c, vw), fwd(0)), pl.BlockSpec((1, c, vw), bwd(0)), st],
        out_shape=[jax.ShapeDtypeStruct((b, l, hv * LANES), F32),
                   jax.ShapeDtypeStruct((b, l, hv * LANES), F32),
                   jax.ShapeDtypeStruct(s0g.shape, F32)],
        scratch_shapes=[pltpu.VMEM((2 * hb * rep, LANES, LANES), F32)],
        compiler_params=_cparams("parallel", "parallel", "arbitrary"),
        name="gdn_core",
    )(qkv, qkv, qkv, ab, qkv, qkv, qkv, ab, alog_row, dtb_row, s0g)
    sg = sg.reshape(b, ngrp, 2, hb * rep, LANES, LANES).transpose(0, 2, 1, 3, 4, 5)
    return of, ob, sg.reshape(b, 2, hv, LANES, LANES)


def _ret_core_kernel(qf_ref, kf_ref, vf_ref, qb_ref, kb_ref, vb_ref, lg_ref, s0_ref,
                     of_ref, ob_ref, sout_ref, s_ref, mask_ref):
    t = pl.program_id(2)
    c = qf_ref.shape[1]
    lgf, lgb = lg_ref[0, 0:1, :], lg_ref[0, 1:2, :]

    @pl.when(t == 0)
    def _():
        s_ref[0] = s0_ref[0, 0, 0]
        s_ref[1] = s0_ref[0, 0, 1]
        rr = lax.broadcasted_iota(jnp.int32, (c, c), 0)
        cc = lax.broadcasted_iota(jnp.int32, (c, c), 1)
        diff = (rr - cc).astype(F32)
        mask_ref[...] = jnp.where(diff > 0, jnp.exp(diff * lgf[:, :c]),
                                  jnp.where(diff < 0, jnp.exp(-diff * lgb[:, :c]), 2.0))

    pos = lax.broadcasted_iota(jnp.int32, (c, LANES), 0).astype(F32)
    lgf1, lgb1 = lgf[:, :LANES], lgb[:, :LANES]
    q, k, v = qf_ref[0], kf_ref[0], vf_ref[0]
    qf32, kf32 = q.astype(F32), k.astype(F32)
    sc = _dot_nt(q, k) * mask_ref[...]
    sf = s_ref[0]
    of_ref[0] = _dot(sc, v) + _dot(qf32 * jnp.exp((pos + 1.0) * lgf1), sf)
    kd = kf32 * jnp.exp((c - 1.0 - pos) * lgf1)
    s_ref[0] = sf * jnp.exp(c * lgf) + _dot(kd.T, v)
    q, k, v = qb_ref[0], kb_ref[0], vb_ref[0]
    sb = s_ref[1]
    ob_ref[0] = _dot(q.astype(F32) * jnp.exp((c - pos) * lgb1), sb)
    kd = k.astype(F32) * jnp.exp(pos * lgb1)
    s_ref[1] = sb * jnp.exp(c * lgb) + _dot(kd.T, v)

    @pl.when(t == pl.num_programs(2) - 1)
    def _():
        sout_ref[0, 0, 0] = s_ref[0]
        sout_ref[0, 0, 1] = s_ref[1]


def _ret_core(qk, pre, lg, s0):
    b, l, _ = qk.shape
    c = min(RET_CHUNK, l)
    nt = l // c
    h_, dk, dv = RET_HEADS, RET_QK_DIM, RET_V_DIM
    fwd = lambda off: (lambda bi, h, t: (bi, t, off + h))
    bwd = lambda off: (lambda bi, h, t: (bi, nt - 1 - t, off + h))
    v_off = 2 * h_ * dk // dv

    def specs(mk):
        return [pl.BlockSpec((1, c, dk), mk(0)), pl.BlockSpec((1, c, dk), mk(h_)),
                pl.BlockSpec((1, c, dv), mk(v_off))]

    st = pl.BlockSpec((1, 1, 2, dk, dv), lambda bi, h, t: (bi, h, 0, 0, 0))
    return pl.pallas_call(
        _ret_core_kernel,
        grid=(b, h_, nt),
        in_specs=specs(fwd) + specs(bwd) + [pl.BlockSpec((1, 2, dv), lambda bi, h, t: (h, 0, 0)), st],
        out_specs=[pl.BlockSpec((1, c, dv), fwd(0)), pl.BlockSpec((1, c, dv), bwd(0)), st],
        out_shape=[jax.ShapeDtypeStruct((b, l, h_ * dv), F32),
                   jax.ShapeDtypeStruct((b, l, h_ * dv), F32),
                   jax.ShapeDtypeStruct((b, h_, 2, dk, dv), F32)],
        scratch_shapes=[pltpu.VMEM((2, dk, dv), F32), pltpu.VMEM((c, c), F32)],
        compiler_params=_cparams("parallel", "parallel", "arbitrary"),
        name="retention_core",
    )(qk, qk, pre, qk, qk, pre, lg, s0)


def _final_norm_kernel(x_ref, g_ref, o_ref):
    x = x_ref[0]
    o_ref[0] = x * lax.rsqrt(jnp.mean(x * x, axis=-1, keepdims=True) + EPS) * g_ref[...]


def _final_norm(x, g, tm=1024):
    b, l, d = x.shape
    tm = min(tm, l)
    return pl.pallas_call(
        _final_norm_kernel,
        grid=(b, l // tm),
        in_specs=[pl.BlockSpec((1, tm, d), lambda bi, i: (bi, i, 0)),
                  pl.BlockSpec((1, d), lambda bi, i: (0, 0))],
        out_specs=pl.BlockSpec((1, tm, d), lambda bi, i: (bi, i, 0)),
        out_shape=jax.ShapeDtypeStruct((b, l, d), F32),
        compiler_params=_cparams("parallel", "parallel"),
        name="final_norm",
    )(x, g.reshape(1, d))


def _attention_layer(x, xc, m, mc, norm_g, w_in, q_g, k_g, w_out, tables, want_ctx):
    hd = ATT_HEADS * ATT_HEAD_DIM
    kvd = ATT_KV_HEADS * ATT_HEAD_DIM
    w_in = w_in.astype(BF16)
    w_out = w_out.astype(BF16)
    gains = jnp.concatenate([jnp.tile(q_g * (ATT_HEAD_DIM ** -0.5 * LOG2E), (ATT_HEADS, 1)),
                             jnp.tile(k_g, (ATT_KV_HEADS, 1))], axis=0)
    n_heads = ATT_HEADS + ATT_KV_HEADS

    def project(t, mm, tabs):
        qkv = _proj(t, norm_g, mm[1], mm[0], w_in, BF16, tm=1024, tn=hd + 2 * kvd)
        return _head_prep(qkv, gains, tabs, norm=True), qkv[:, :, hd + kvd:]

    qk, v = project(x, m, tables)
    qkc, vc = project(xc, mc, None)
    kv_c = jnp.concatenate([qkc[:, :, hd:], vc], axis=-1)
    kv_all = jnp.concatenate([jnp.concatenate([qk[:, :, hd:], v], axis=-1), kv_c], axis=1)
    o = _attention(qk, kv_all)
    tm = min(512, x.shape[1])
    x = _out_res(_plain_prologue, [o], [_row_spec(tm, hd)], w_out, x, m[2], tm)
    if want_ctx:
        oc = _attention(qkc, kv_c)
        tmc = min(512, xc.shape[1])
        xc = _out_res(_plain_prologue, [oc], [_row_spec(tmc, hd)], w_out, xc, mc[2], tmc)
    return x, xc


def _gdn_layer(x, xc, m, mc, norm_g, w_in, conv_w, a_log, dt_bias, gn_g, w_out, want_ctx):
    hq, hv, dh = GDN_QK_HEADS, GDN_V_HEADS, GDN_HEAD_DIM
    cw = (2 * hq + hv) * dh
    zw = hv * dh
    w_main = w_in[:, :cw + zw].astype(BF16)
    w_ab = jnp.pad(w_in[:, cw + zw:], ((0, 0), (0, LANES - 4 * hv))).astype(BF16)
    w_out = w_out.astype(BF16)
    pad = (0, LANES - 2 * hv)
    alog_row = jnp.pad(a_log.reshape(-1), pad).reshape(1, LANES)
    dtb_row = jnp.pad(dt_bias.reshape(-1), pad).reshape(1, LANES)

    def project(t, mm):
        pre = _proj(t, norm_g, mm[1], mm[0], w_main, BF16, tm=1024, tn=2048)
        ab = _proj(t, norm_g, mm[1], mm[0], w_ab, F32, tm=1024, tn=LANES)
        return _gdn_conv(pre, conv_w, n_norm=2 * hq), pre, ab

    b = x.shape[0]
    qkv_c, pre_c, ab_c = project(xc, mc)
    qkv, pre, ab = project(x, m)
    s0 = jnp.zeros((b, 2, hv, dh, dh), F32)
    ocf, ocb, sc = _gdn_core(qkv_c, ab_c, alog_row, dtb_row, s0)
    of, ob, _ = _gdn_core(qkv, ab, alog_row, dtb_row, sc)

    def out(of_, ob_, pre_, t, mm):
        tm = min(512, t.shape[1])
        zspec = _row_spec(tm, zw, col=cw // zw)
        gspec = pl.BlockSpec((1, LANES), lambda bi, i: (0, 0))
        return _out_res(_gdn_prologue, [of_, ob_, pre_, gn_g.reshape(1, dh)],
                        [_row_spec(tm, zw), _row_spec(tm, zw), zspec, gspec], w_out, t, mm[2], tm)

    x = out(of, ob, pre, x, m)
    if want_ctx:
        xc = out(ocf, ocb, pre_c, xc, mc)
    return x, xc


def _retention_layer(x, xc, m, mc, norm_g, w_in, decay, w_out, tables, want_ctx):
    h_, dk, dv = RET_HEADS, RET_QK_DIM, RET_V_DIM
    qkw = 2 * h_ * dk
    w_in = w_in.astype(BF16)
    w_out = w_out.astype(BF16)
    gains = jnp.concatenate([jnp.full((h_, dk), dk ** -0.5, F32), jnp.ones((h_, dk), F32)], axis=0)
    lg = jnp.broadcast_to(-decay.astype(F32).T[:, :, None], (h_, 2, dv))

    def project(t, mm, tabs):
        pre = _proj(t, norm_g, mm[1], mm[0], w_in, BF16, tm=1024, tn=2048)
        return _head_prep(pre, gains, tabs, norm=False), pre

    b = x.shape[0]
    qk_c, vg_c = project(xc, mc, None)
    qk, vg = project(x, m, tables)
    s0 = jnp.zeros((b, h_, 2, dk, dv), F32)
    ocf, ocb, sc = _ret_core(qk_c, vg_c, lg, s0)
    of, ob, _ = _ret_core(qk, vg, lg, sc)
    vw = h_ * dv

    def out(of_, ob_, vg_, t, mm):
        tm = min(512, t.shape[1])
        return _out_res(_ret_prologue, [of_, ob_, vg_],
                        [_row_spec(tm, vw), _row_spec(tm, vw), _row_spec(tm, vw, col=2)],
                        w_out, t, mm[2], tm)

    x = out(of, ob, vg, x, m)
    if want_ctx:
        xc = out(ocf, ocb, vg_c, xc, mc)
    return x, xc


def _ffn(x, m, norm_g, w_in, conv_w, conv_b, w_out):
    f = w_out.shape[0]
    u = _proj(x, norm_g, m[4], m[3], w_in.astype(BF16), BF16, tm=1024, tn=f)
    return _ffn_out(u, conv_w, conv_b, w_out.astype(BF16), x, m[5])


def kernel(x, c, ctx, c_ctx, mod_w, mod_b, norm1_g, norm2_g, att_w_in, att_q_g, att_k_g, att_w_out, gdn_w_in, gdn_conv_w, gdn_A_log, gdn_dt_bias, gdn_norm_g, gdn_w_out, ret_w_in, ret_decay, ret_w_out, ffn_w_in, ffn_conv_w, ffn_conv_b, ffn_w_out, final_g):
    b, n, d = x.shape
    depth = mod_w.shape[0]
    cc = jnp.concatenate([c, c_ctx[None, :], jnp.zeros((8 - b - 1, d), F32)], axis=0)
    mods = _modulation(cc, mod_w, mod_b)
    tab_att = _rope_tables(n, ATT_HEAD_DIM)
    tab_ret = _rope_tables(n, RET_QK_DIM)
    xc = ctx
    for i in range(depth):
        last = i == depth - 1
        mi = mods[i].reshape(8, 6, d)
        m = [mi[:b, s][:, None, :] for s in range(6)]
        mc = [jnp.broadcast_to(mi[b, s][None, None, :], (b, 1, d)) for s in range(6)]
        kind, j = i % N_MIXERS, i // N_MIXERS
        if kind == 0:
            x, xc = _attention_layer(x, xc, m, mc, norm1_g[i], att_w_in[j], att_q_g[j], att_k_g[j],
                                     att_w_out[j], tab_att, not last)
        elif kind == 1:
            x, xc = _gdn_layer(x, xc, m, mc, norm1_g[i], gdn_w_in[j], gdn_conv_w[j], gdn_A_log[j],
                               gdn_dt_bias[j], gdn_norm_g[j], gdn_w_out[j], not last)
        else:
            x, xc = _retention_layer(x, xc, m, mc, norm1_g[i], ret_w_in[j], ret_decay[j],
                                     ret_w_out[j], tab_ret, not last)
        x = _ffn(x, m, norm2_g[i], ffn_w_in[i], ffn_conv_w[i], ffn_conv_b[i], ffn_w_out[i])
        if not last:
            xc = _ffn(xc, mc, norm2_g[i], ffn_w_in[i], ffn_conv_w[i], ffn_conv_b[i], ffn_w_out[i])
    return _final_norm(x, final_g)
```

```python
import functools

import jax
import jax.numpy as jnp
from jax import lax
from jax.experimental import pallas as pl
from jax.experimental.pallas import tpu as pltpu

F32 = jnp.float32
BF16 = jnp.bfloat16

EPS = 1e-6
LOG2E = 1.4426950408889634
ROPE_THETA = 10000.0
GRID_W = 64
N_MIXERS = 3

ATT_HEADS, ATT_KV_HEADS, ATT_HEAD_DIM = 8, 2, 128
GDN_QK_HEADS, GDN_V_HEADS, GDN_HEAD_DIM, GDN_CONV_W = 8, 16, 128, 5
RET_HEADS, RET_QK_DIM, RET_V_DIM = 8, 128, 256
FFN_CONV_W = 3

LANES = 128
SUBLANES = 8
BF16_ROWS = 16
MXU_DIM = 256
GDN_CHUNK = 128
GDN_INV_BLOCK = 16
GDN_HEADS_PER_STEP = 4
RET_CHUNK = 256
ATT_MAX_UNROLL = 11
VMEM_LIMIT = 56 * 1024 * 1024


def _cparams(*sem):
    return pltpu.CompilerParams(dimension_semantics=sem, vmem_limit_bytes=VMEM_LIMIT)


def _sigmoid(x):
    return 1.0 / (1.0 + jnp.exp(-x))


def _silu(x):
    return x * _sigmoid(x)


def _dot(a, b):
    return jnp.dot(a.astype(BF16), b.astype(BF16), preferred_element_type=F32)


def _dot_nt(a, b):
    return lax.dot_general(a.astype(BF16), b.astype(BF16), (((1,), (1,)), ((), ())),
                           preferred_element_type=F32)


def _mod_kernel(cc_ref, w_ref, b_ref, o_ref):
    s = _silu(cc_ref[...])
    o_ref[0] = _dot(s, w_ref[0]) + b_ref[0]


def _modulation(cc, mod_w, mod_b):
    depth, d, n6 = mod_w.shape
    tn = n6 // 4
    return pl.pallas_call(
        _mod_kernel,
        grid=(depth, n6 // tn),
        in_specs=[pl.BlockSpec((8, d), lambda l, j: (0, 0)),
                  pl.BlockSpec((1, d, tn), lambda l, j: (l, 0, j)),
                  pl.BlockSpec((1, 1, tn), lambda l, j: (l, 0, j))],
        out_specs=pl.BlockSpec((1, 8, tn), lambda l, j: (l, 0, j)),
        out_shape=jax.ShapeDtypeStruct((depth, 8, n6), F32),
        compiler_params=_cparams("parallel", "parallel"),
        name="modulation",
    )(cc, mod_w, mod_b.reshape(depth, 1, n6))


def _norm_mod(x, g, sc, sh):
    ms = jnp.mean(x * x, axis=-1, keepdims=True)
    return (x * lax.rsqrt(ms + EPS)) * g * (1.0 + sc) + sh


def _proj_kernel(x_ref, g_ref, sc_ref, sh_ref, w_ref, o_ref, h_ref):
    @pl.when(pl.program_id(2) == 0)
    def _():
        h_ref[...] = _norm_mod(x_ref[0], g_ref[...], sc_ref[0], sh_ref[0]).astype(BF16)

    o_ref[0] = jnp.dot(h_ref[...], w_ref[...], preferred_element_type=F32).astype(o_ref.dtype)


def _proj(x, g, sc, sh, w, out_dtype, tm, tn):
    b, l, d = x.shape
    n = w.shape[1]
    tm = min(tm, l)
    tn = min(tn, n)
    return pl.pallas_call(
        _proj_kernel,
        grid=(b, l // tm, n // tn),
        in_specs=[pl.BlockSpec((1, tm, d), lambda bi, i, j: (bi, i, 0)),
                  pl.BlockSpec((1, d), lambda bi, i, j: (0, 0)),
                  pl.BlockSpec((1, 1, d), lambda bi, i, j: (bi, 0, 0)),
                  pl.BlockSpec((1, 1, d), lambda bi, i, j: (bi, 0, 0)),
                  pl.BlockSpec((d, tn), lambda bi, i, j: (0, j))],
        out_specs=pl.BlockSpec((1, tm, tn), lambda bi, i, j: (bi, i, j)),
        out_shape=jax.ShapeDtypeStruct((b, l, n), out_dtype),
        scratch_shapes=[pltpu.VMEM((tm, d), BF16)],
        compiler_params=_cparams("parallel", "parallel", "arbitrary"),
        name="norm_mod_proj",
    )(x, g.reshape(1, d), sc, sh, w)


def _rope_tables(n, head_dim):
    rows = n // GRID_W
    row = jnp.repeat(jnp.arange(rows, dtype=F32), GRID_W)
    col = jnp.tile(jnp.arange(GRID_W, dtype=F32), rows)
    axis_dim = head_dim // 2
    inv_freq = ROPE_THETA ** (-jnp.arange(0, axis_dim, 2, dtype=F32) / axis_dim)
    ar, ac = row[:, None] * inv_freq, col[:, None] * inv_freq
    cr, sr, cc, sc = jnp.cos(ar), jnp.sin(ar), jnp.cos(ac), jnp.sin(ac)
    z = jnp.zeros_like(sr)
    return (jnp.concatenate([cr, cr, cc, cc], axis=-1),
            jnp.concatenate([-sr, z, -sc, z], axis=-1),
            jnp.concatenate([z, sr, z, sc], axis=-1))


def _head_prep_kernel(*refs, norm, rope):
    if rope:
        x_ref, g_ref, c_ref, sa_ref, sb_ref, o_ref = refs
    else:
        x_ref, g_ref, o_ref = refs
    for h in range(g_ref.shape[1]):
        sl = slice(h * LANES, (h + 1) * LANES)
        x = x_ref[0, :, sl].astype(F32)
        if norm:
            x = x * lax.rsqrt(jnp.mean(x * x, axis=-1, keepdims=True) + EPS)
        x = x * g_ref[0, h:h + 1, :]
        if rope:
            x = (x * c_ref[...] + pltpu.roll(x, 96, 1) * sa_ref[...]
                 + pltpu.roll(x, 32, 1) * sb_ref[...])
        o_ref[0, :, sl] = x.astype(o_ref.dtype)


def _head_prep(x, gains, tables, norm, tm=512):
    b, l, _ = x.shape
    nb = gains.shape[0]
    tm = min(tm, l)
    hw = max(h for h in (8, 5, 4, 2, 1) if nb % h == 0)
    in_specs = [pl.BlockSpec((1, tm, hw * LANES), lambda bi, i, c: (bi, i, c)),
                pl.BlockSpec((1, hw, LANES), lambda bi, i, c: (c, 0, 0))]
    args = [x, gains.reshape(nb // hw, hw, LANES)]
    if tables is not None:
        in_specs += [pl.BlockSpec((tm, LANES), lambda bi, i, c: (i, 0))] * 3
        args += list(tables)
    return pl.pallas_call(
        functools.partial(_head_prep_kernel, norm=norm, rope=tables is not None),
        grid=(b, l // tm, nb // hw),
        in_specs=in_specs,
        out_specs=pl.BlockSpec((1, tm, hw * LANES), lambda bi, i, c: (bi, i, c)),
        out_shape=jax.ShapeDtypeStruct((b, l, nb * LANES), BF16),
        compiler_params=_cparams("parallel", "parallel", "parallel"),
        name="head_prep",
    )(*args)


def _attn_kernel(q_ref, k_ref, vt_ref, o_ref, *, groups, tq, unroll):
    dh = LANES
    qt = jnp.concatenate([q_ref[0, :, g * dh:(g + 1) * dh].astype(F32).T for g in range(groups)],
                         axis=1).astype(BF16)
    rows = groups * tq
    nk, vrows, tk = vt_ref.shape[2:]

    def scores(j):
        off = pl.multiple_of(j * tk, tk)
        return jnp.dot(k_ref[0, pl.ds(off, tk), :], qt, preferred_element_type=F32)

    def body(it, carry):
        m, acc = carry
        base = it * unroll
        st = scores(base)
        pv = alpha = None
        for u in range(unroll):
            st_next = scores(base + u + 1) if u + 1 < unroll else None
            m_new = jnp.maximum(m, jnp.max(st, axis=0, keepdims=True))
            p = jnp.exp2(st - m_new).astype(BF16)
            if pv is not None:
                acc = alpha * acc + pv
            alpha = jnp.exp2(m - m_new)
            pv = jnp.dot(vt_ref[0, 0, base + u], p, preferred_element_type=F32)
            m, st = m_new, st_next
        return m, alpha * acc + pv

    init = (jnp.full((1, rows), -1e30, F32), jnp.zeros((vrows, rows), F32))
    _, acc = lax.fori_loop(0, nk // unroll, body, init)
    out = acc[:dh] / acc[dh:dh + 1]
    for g in range(groups):
        o_ref[0, :, g * dh:(g + 1) * dh] = out[:, g * tq:(g + 1) * tq].T.astype(o_ref.dtype)


def _attention(q, kv, tq=256, tk=256):
    b, n, hd = q.shape
    lk = kv.shape[1]
    kvh = ATT_KV_HEADS
    groups = ATT_HEADS // kvh
    gw = groups * LANES
    tq = min(tq, n)
    tk = min(tk, lk)
    nk = lk // tk
    vt = kv[:, :, kvh * LANES:].reshape(b, nk, tk, kvh, LANES).transpose(0, 3, 1, 4, 2)
    vt = jnp.concatenate([vt, jnp.ones((b, kvh, nk, BF16_ROWS, tk), BF16)], axis=3)
    vrows = LANES + BF16_ROWS
    unroll = max(u for u in range(1, ATT_MAX_UNROLL + 1) if nk % u == 0)
    return pl.pallas_call(
        functools.partial(_attn_kernel, groups=groups, tq=tq, unroll=unroll),
        grid=(b, kvh, n // tq),
        in_specs=[pl.BlockSpec((1, tq, gw), lambda bi, h, i: (bi, i, h)),
                  pl.BlockSpec((1, lk, LANES), lambda bi, h, i: (bi, 0, h)),
                  pl.BlockSpec((1, 1, nk, vrows, tk), lambda bi, h, i: (bi, h, 0, 0, 0))],
        out_specs=pl.BlockSpec((1, tq, gw), lambda bi, h, i: (bi, i, h)),
        out_shape=jax.ShapeDtypeStruct((b, n, hd), BF16),
        compiler_params=_cparams("parallel", "parallel", "parallel"),
        name="gqa_attention",
    )(q, kv, vt)


def _plain_prologue(a_ref):
    return a_ref[0]


def _gdn_prologue(of_ref, ob_ref, z_ref, g_ref):
    outs = []
    g = g_ref[...]
    for h in range(GDN_V_HEADS):
        sl = slice(h * LANES, (h + 1) * LANES)
        o = of_ref[0, :, sl].astype(F32) + ob_ref[0, :, sl].astype(F32)
        o = o * lax.rsqrt(jnp.mean(o * o, axis=-1, keepdims=True) + EPS) * g
        outs.append((o * _silu(z_ref[0, :, sl].astype(F32))).astype(BF16))
    return jnp.concatenate(outs, axis=-1)


def _ret_prologue(of_ref, ob_ref, gate_ref):
    outs = []
    for h in range(RET_HEADS):
        sl = slice(h * RET_V_DIM, (h + 1) * RET_V_DIM)
        o = of_ref[0, :, sl].astype(F32) + ob_ref[0, :, sl].astype(F32)
        o = o * lax.rsqrt(jnp.mean(o * o, axis=-1, keepdims=True) + EPS)
        outs.append((o * _silu(gate_ref[0, :, sl].astype(F32))).astype(BF16))
    return jnp.concatenate(outs, axis=-1)


def _out_res_kernel(*refs, prologue, n_in):
    w_ref, x_ref, ga_ref, o_ref = refs[n_in:]
    a = prologue(*refs[:n_in])
    y = jnp.dot(a, w_ref[...], preferred_element_type=F32)
    o_ref[0] = x_ref[0] + ga_ref[0] * y


def _out_res(prologue, ins, in_specs, w, x, ga, tm):
    b, l, d = x.shape
    k = w.shape[0]
    return pl.pallas_call(
        functools.partial(_out_res_kernel, prologue=prologue, n_in=len(ins)),
        grid=(b, l // tm),
        in_specs=in_specs + [pl.BlockSpec((k, d), lambda bi, i: (0, 0)),
                             pl.BlockSpec((1, tm, d), lambda bi, i: (bi, i, 0)),
                             pl.BlockSpec((1, 1, d), lambda bi, i: (bi, 0, 0))],
        out_specs=pl.BlockSpec((1, tm, d), lambda bi, i: (bi, i, 0)),
        out_shape=jax.ShapeDtypeStruct((b, l, d), F32),
        compiler_params=_cparams("parallel", "parallel"),
        name="out_proj_residual",
    )(*ins, w, x, ga)


def _row_spec(tm, width, col=0):
    return pl.BlockSpec((1, tm, width), lambda bi, i: (bi, i, col))


def _halo_extend(x_ref, prev_ref, next_ref, sl, keep_prev, keep_next):
    x = x_ref[0, :, sl].astype(F32)
    prev = jnp.where(keep_prev, prev_ref[0, :, sl].astype(F32)[BF16_ROWS - SUBLANES:], 0.0)
    nxt = jnp.where(keep_next, next_ref[0, :, sl].astype(F32)[:SUBLANES], 0.0)
    return x, jnp.concatenate([prev, x, nxt], axis=0)


def _ffn_out_kernel(val_ref, gate_ref, prev_ref, next_ref, cw_ref, cb_ref, w_ref, x_ref, ga_ref,
                    o_ref, *, sub):
    i = pl.program_id(1)
    tm, f = gate_ref.shape[1:]
    keep_prev, keep_next = i > 0, i < pl.num_programs(1) - 1

    def act(j):
        sl = slice(j * sub, (j + 1) * sub)
        g, ge = _halo_extend(gate_ref, prev_ref, next_ref, sl, keep_prev, keep_next)
        cw = cw_ref[:, sl]
        conv = (cw[0:1] * ge[SUBLANES - 1:SUBLANES - 1 + tm] + cw[1:2] * g
                + cw[2:3] * ge[SUBLANES + 1:SUBLANES + 1 + tm] + cb_ref[:, sl])
        gelu = 0.5 * conv * (1.0 + lax.erf(conv * (2.0 ** -0.5)))
        return (gelu * val_ref[0, :, sl].astype(F32)).astype(BF16)

    n = f // sub
    a, y = act(0), None
    for j in range(n):
        a_next = act(j + 1) if j + 1 < n else None
        part = jnp.dot(a, w_ref[j * sub:(j + 1) * sub, :], preferred_element_type=F32)
        y = part if y is None else y + part
        a = a_next
    o_ref[0] = x_ref[0] + ga_ref[0] * y


def _ffn_out(u, conv_w, conv_b, w_out, x, ga, tm=512):
    b, l, d = x.shape
    f = w_out.shape[0]
    tm = min(tm, l)
    sub = MXU_DIM if f % MXU_DIM == 0 else LANES
    hb = tm // BF16_ROWS
    last_hb = l // BF16_ROWS - 1
    return pl.pallas_call(
        functools.partial(_ffn_out_kernel, sub=sub),
        grid=(b, l // tm),
        in_specs=[pl.BlockSpec((1, tm, f), lambda bi, i: (bi, i, 0)),
                  pl.BlockSpec((1, tm, f), lambda bi, i: (bi, i, 1)),
                  pl.BlockSpec((1, BF16_ROWS, f), lambda bi, i: (bi, jnp.maximum(i * hb - 1, 0), 1)),
                  pl.BlockSpec((1, BF16_ROWS, f),
                               lambda bi, i: (bi, jnp.minimum((i + 1) * hb, last_hb), 1)),
                  pl.BlockSpec((FFN_CONV_W, f), lambda bi, i: (0, 0)),
                  pl.BlockSpec((1, f), lambda bi, i: (0, 0)),
                  pl.BlockSpec((f, d), lambda bi, i: (0, 0)),
                  pl.BlockSpec((1, tm, d), lambda bi, i: (bi, i, 0)),
                  pl.BlockSpec((1, 1, d), lambda bi, i: (bi, 0, 0))],
        out_specs=pl.BlockSpec((1, tm, d), lambda bi, i: (bi, i, 0)),
        out_shape=jax.ShapeDtypeStruct((b, l, d), F32),
        compiler_params=_cparams("parallel", "parallel"),
        name="convglu_out",
    )(u, u, u, u, conv_w, conv_b.reshape(1, f), w_out, x, ga)


def _gdn_conv_kernel(x_ref, prev_ref, next_ref, cw_ref, o_ref, *, n_norm, hw):
    i, c = pl.program_id(1), pl.program_id(2)
    tm = x_ref.shape[1]
    keep_prev, keep_next = i > 0, i < pl.num_programs(1) - 1
    pad = GDN_CONV_W // 2
    for h in range(hw):
        sl = slice(h * LANES, (h + 1) * LANES)
        _, xe = _halo_extend(x_ref, prev_ref, next_ref, sl, keep_prev, keep_next)
        cw = cw_ref[:, sl]
        y = cw[0:1] * xe[SUBLANES - pad:SUBLANES - pad + tm]
        for j in range(1, GDN_CONV_W):
            y = y + cw[j:j + 1] * xe[SUBLANES - pad + j:SUBLANES - pad + j + tm]
        y = _silu(y)
        inv = lax.rsqrt(jnp.sum(y * y, axis=-1, keepdims=True) + EPS)
        o_ref[0, :, sl] = (y * jnp.where(c * hw + h < n_norm, inv, 1.0)).astype(o_ref.dtype)


def _gdn_conv(pre, conv_w, n_norm, tm=512, hw=4):
    b, l, _ = pre.shape
    ch = conv_w.shape[1]
    tm = min(tm, l)
    hb = tm // BF16_ROWS
    last_hb = l // BF16_ROWS - 1
    bw = hw * LANES
    return pl.pallas_call(
        functools.partial(_gdn_conv_kernel, n_norm=n_norm, hw=hw),
        grid=(b, l // tm, ch // bw),
        in_specs=[pl.BlockSpec((1, tm, bw), lambda bi, i, c: (bi, i, c)),
                  pl.BlockSpec((1, BF16_ROWS, bw),
                               lambda bi, i, c: (bi, jnp.maximum(i * hb - 1, 0), c)),
                  pl.BlockSpec((1, BF16_ROWS, bw),
                               lambda bi, i, c: (bi, jnp.minimum((i + 1) * hb, last_hb), c)),
                  pl.BlockSpec((GDN_CONV_W, bw), lambda bi, i, c: (0, c))],
        out_specs=pl.BlockSpec((1, tm, bw), lambda bi, i, c: (bi, i, c)),
        out_shape=jax.ShapeDtypeStruct((b, l, ch), BF16),
        compiler_params=_cparams("parallel", "parallel", "parallel"),
        name="gdn_conv",
    )(pre, pre, pre, conv_w)


def _mm(xs, ys):
    return [jnp.dot(x, y, preferred_element_type=F32) for x, y in zip(xs, ys)]


def _bf(xs):
    return [x.astype(BF16) for x in xs]


def _neumann_minus_eye(ms, order):
    mb = _bf(ms)
    m2 = _mm(mb, mb)
    pb = _bf(m2)
    m3 = _mm(mb, pb)
    r = [b - a - c for a, b, c in zip(ms, m2, m3)]
    p, mp = 4, m2
    while p < order:
        mp = _mm(pb, pb)
        pb = _bf(mp)
        rm = _mm(_bf(r), pb)
        r = [x + y + z for x, y, z in zip(r, mp, rm)]
        p *= 2
    return r


def _unit_tri_inverse_minus_eye(a_list, same_block):
    c = a_list[0].shape[0]
    d = [jnp.where(same_block, a, 0.0) for a in a_list]
    lo = [a - x for a, x in zip(a_list, d)]
    r0 = _neumann_minus_eye(d, GDN_INV_BLOCK)
    r0b = _bf(r0)
    n = [x + y for x, y in zip(lo, _mm(r0b, _bf(lo)))]
    rn = _neumann_minus_eye(n, c // GDN_INV_BLOCK)
    return [x + y + z for x, y, z in zip(r0, rn, _mm(_bf(rn), r0b))]


def _gdn_core_kernel(qf_ref, kf_ref, vf_ref, abf_ref, qb_ref, kb_ref, vb_ref, abb_ref,
                     alog_ref, dtb_ref, s0_ref, of_ref, ob_ref, sout_ref, s_ref, gt_ref, *, hb):
    t = pl.program_id(2)
    grp = pl.program_id(1)
    rep = GDN_V_HEADS // GDN_QK_HEADS
    c, dk = GDN_CHUNK, GDN_HEAD_DIM
    scale = dk ** -0.5

    @pl.when(t == 0)
    def _():
        s_ref[...] = s0_ref[0]

    rr = lax.broadcasted_iota(jnp.int32, (c, c), 0)
    cc = lax.broadcasted_iota(jnp.int32, (c, c), 1)
    same_block = (rr // GDN_INV_BLOCK) == (cc // GDN_INV_BLOCK)
    lane = lax.broadcasted_iota(jnp.int32, (c, LANES), 1)
    ng = 2 * GDN_V_HEADS
    dirs = ((qf_ref, kf_ref, vf_ref, abf_ref, of_ref, rr >= cc, rr > cc),
            (qb_ref, kb_ref, vb_ref, abb_ref, ob_ref, rr <= cc, rr < cc))

    gcs, balls, kks, qks, kts = [], [], [], [], []
    for q_ref, k_ref, _, ab_ref, _, incl, _ in dirs:
        ab = ab_ref[0]
        sp = ab + dtb_ref[...]
        sp = jnp.maximum(sp, 0.0) + jnp.log1p(jnp.exp(-jnp.abs(sp)))
        gall = -jnp.exp(alog_ref[...]) * sp
        balls.append(_sigmoid(ab))
        g1 = gall.astype(BF16)
        r1 = gall - g1.astype(F32)
        g2 = r1.astype(BF16)
        g3 = (r1 - g2.astype(F32)).astype(BF16)
        packed = jnp.where(lane < ng, g1.astype(F32),
                           jnp.where(lane < 2 * ng, pltpu.roll(g2.astype(F32), ng, 1),
                                     jnp.where(lane < 3 * ng,
                                               pltpu.roll(g3.astype(F32), 2 * ng, 1), 0.0)))
        sel = jnp.concatenate([jnp.where(incl, 1.0, 0.0), jnp.ones((c, c), F32)], axis=0)
        cs = _dot(sel, packed)
        cs = cs + pltpu.roll(cs, LANES - ng, 1) + pltpu.roll(cs, LANES - 2 * ng, 1)
        gcs.append(cs[:c])
        d = len(gcs) - 1
        gt_ref[d, 0] = cs[:c].T
        gt_ref[d, 1] = cs[c:].T
        for hh in range(hb):
            sl = slice(hh * dk, (hh + 1) * dk)
            k = k_ref[0, :, sl]
            g = _dot_nt(jnp.concatenate([k, q_ref[0, :, sl]], axis=0), k)
            kks.append(g[:c])
            qks.append(g[c:])
            kts.append(k.astype(F32).T)

    chains = [(d, hh, e) for d in range(2) for hh in range(hb) for e in range(rep)]
    a_list, xs, qkms, qds, kdts, gls = [], [], [], [], [], []
    for d, hh, e in chains:
        q_ref, k_ref, v_ref, _, _, incl, strict = dirs[d]
        col = d * GDN_V_HEADS + (grp * hb + hh) * rep + e
        pick = lambda m, j: jnp.sum(jnp.where(lane == j, m, 0.0), axis=-1, keepdims=True)
        gc = pick(gcs[d], col)
        bcol = pick(balls[d], col + ng)
        gc_row = gt_ref[d, 0, pl.ds(col, 1), :]
        gtot_row = gt_ref[d, 1, pl.ds(col, 1), :]
        decay = jnp.where(incl, jnp.exp(jnp.minimum(gc - gc_row, 0.0)), 0.0)
        a_list.append(jnp.where(strict, bcol * kks[d * hb + hh] * decay, 0.0))
        kf = k_ref[0, :, hh * dk:(hh + 1) * dk].astype(F32)
        qf = q_ref[0, :, hh * dk:(hh + 1) * dk].astype(F32)
        vf = v_ref[0, :, (hh * rep + e) * dk:(hh * rep + e + 1) * dk].astype(F32)
        egc = jnp.exp(gc)
        xs.append(jnp.concatenate([vf * bcol, kf * (bcol * egc)], axis=1))
        qkms.append((qks[d * hb + hh] * decay * scale).astype(BF16))
        qds.append(qf * (egc * scale))
        kdts.append((kts[d * hb + hh] * jnp.exp(gtot_row - gc_row)).astype(BF16))
        gls.append(jnp.exp(gtot_row[:, :dk]))

    tms = _unit_tri_inverse_minus_eye(a_list, same_block)
    uws = [x + y for x, y in zip(xs, _mm(_bf(tms), _bf(xs)))]

    ss = [s_ref[i] for i in range(len(chains))]
    sbs = _bf(ss)
    wss = _mm([jnp.concatenate([uw[:, dk:], qd], axis=0).astype(BF16) for uw, qd in zip(uws, qds)],
              sbs)
    vns = [(uw[:, :dk] - ws[:c]).astype(BF16) for uw, ws in zip(uws, wss)]
    ovs = _mm(qkms, vns)
    kvs = _mm(kdts, vns)
    for i, (d, hh, e) in enumerate(chains):
        o_ref = dirs[d][4]
        o_ref[0, :, (hh * rep + e) * dk:(hh * rep + e + 1) * dk] = (
            wss[i][c:] + ovs[i]).astype(o_ref.dtype)
        s_ref[i] = ss[i] * gls[i] + kvs[i]

    @pl.when(t == pl.num_programs(2) - 1)
    def _():
        sout_ref[0] = s_ref[...]


def _gdn_core(qkv, ab, alog_row, dtb_row, s0, hb=GDN_HEADS_PER_STEP):
    b, l, _ = qkv.shape
    c = GDN_CHUNK
    nt = l // c
    hq, hv = GDN_QK_HEADS, GDN_V_HEADS
    rep = hv // hq
    ngrp = hq // hb
    fwd = lambda off: (lambda bi, g, t: (bi, t, off + g))
    bwd = lambda off: (lambda bi, g, t: (bi, nt - 1 - t, off + g))
    qw, vw = hb * LANES, hb * rep * LANES

    def specs(mk, ab_map):
        return [pl.BlockSpec((1, c, qw), mk(0)),
                pl.BlockSpec((1, c, qw), mk(ngrp)),
                pl.BlockSpec((1, c, vw), mk(ngrp)),
                pl.BlockSpec((1, c, LANES), ab_map)]

    ab_fwd = lambda bi, g, t: (bi, t, 0)
    ab_bwd = lambda bi, g, t: (bi, nt - 1 - t, 0)
    row = pl.BlockSpec((1, LANES), lambda bi, g, t: (0, 0))
    st = pl.BlockSpec((1, 2 * hb * rep, LANES, LANES), lambda bi, g, t: (bi * ngrp + g, 0, 0, 0))
    s0g = s0.reshape(b, 2, ngrp, hb * rep, LANES, LANES).transpose(0, 2, 1, 3, 4, 5)
    s0g = s0g.reshape(b * ngrp, 2 * hb * rep, LANES, LANES)
    of, ob, sg = pl.pallas_call(
        functools.partial(_gdn_core_kernel, hb=hb),
        grid=(b, ngrp, nt),
        in_specs=specs(fwd, ab_fwd) + specs(bwd, ab_bwd) + [row, row, st],
        out_specs=[pl.BlockSpec((1, c, vw), fwd(0)), pl.BlockSpec((1, c, vw), bwd(0)), st],
        out_shape=[jax.ShapeDtypeStruct((b, l, hv * LANES), BF16),
                   jax.ShapeDtypeStruct((b, l, hv * LANES), BF16),
                   jax.ShapeDtypeStruct(s0g.shape, F32)],
        scratch_shapes=[pltpu.VMEM((2 * hb * rep, LANES, LANES), F32),
                        pltpu.VMEM((2, 2, LANES, c), F32)],
        compiler_params=_cparams("parallel", "parallel", "arbitrary"),
        name="gdn_core",
    )(qkv, qkv, qkv, ab, qkv, qkv, qkv, ab, alog_row, dtb_row, s0g)
    sg = sg.reshape(b, ngrp, 2, hb * rep, LANES, LANES).transpose(0, 2, 1, 3, 4, 5)
    return of, ob, sg.reshape(b, 2, hv, LANES, LANES)


def _ret_core_kernel(qf_ref, kf_ref, vf_ref, qb_ref, kb_ref, vb_ref, lg_ref, s0_ref,
                     of_ref, ob_ref, sout_ref, s_ref, mask_ref, dec_ref):
    t = pl.program_id(2)
    c = qf_ref.shape[1]
    lgf, lgb = lg_ref[0, 0:1, :], lg_ref[0, 1:2, :]

    @pl.when(t == 0)
    def _():
        s_ref[0] = s0_ref[0, 0, 0]
        s_ref[1] = s0_ref[0, 0, 1]
        rr = lax.broadcasted_iota(jnp.int32, (c, c), 0)
        cc = lax.broadcasted_iota(jnp.int32, (c, c), 1)
        diff = (rr - cc).astype(F32)
        mask_ref[...] = jnp.where(diff > 0, jnp.exp(diff * lgf[:, :c]),
                                  jnp.where(diff < 0, jnp.exp(-diff * lgb[:, :c]), 2.0))
        pos = lax.broadcasted_iota(jnp.int32, (c, LANES), 0).astype(F32)
        lgf1, lgb1 = lgf[:, :LANES], lgb[:, :LANES]
        dec_ref[0] = jnp.exp((pos + 1.0) * lgf1)
        dec_ref[1] = jnp.exp((c - 1.0 - pos) * lgf1)
        dec_ref[2] = jnp.exp((c - pos) * lgb1)
        dec_ref[3] = jnp.exp(pos * lgb1)

    q, k, v = qf_ref[0], kf_ref[0], vf_ref[0]
    qb, kb, vb = qb_ref[0], kb_ref[0], vb_ref[0]
    sf, sb = s_ref[0], s_ref[1]
    sc = _dot_nt(q, k)
    inter_f = _dot(q.astype(F32) * dec_ref[0], sf)
    kv_f = _dot((k.astype(F32) * dec_ref[1]).T, v)
    inter_b = _dot(qb.astype(F32) * dec_ref[2], sb)
    kv_b = _dot((kb.astype(F32) * dec_ref[3]).T, vb)
    of_ref[0] = (_dot(sc * mask_ref[...], v) + inter_f).astype(of_ref.dtype)
    ob_ref[0] = inter_b.astype(ob_ref.dtype)
    s_ref[0] = sf * jnp.exp(c * lgf) + kv_f
    s_ref[1] = sb * jnp.exp(c * lgb) + kv_b

    @pl.when(t == pl.num_programs(2) - 1)
    def _():
        sout_ref[0, 0, 0] = s_ref[0]
        sout_ref[0, 0, 1] = s_ref[1]


def _ret_core(qk, pre, lg, s0):
    b, l, _ = qk.shape
    c = min(RET_CHUNK, l)
    nt = l // c
    h_, dk, dv = RET_HEADS, RET_QK_DIM, RET_V_DIM
    fwd = lambda off: (lambda bi, h, t: (bi, t, off + h))
    bwd = lambda off: (lambda bi, h, t: (bi, nt - 1 - t, off + h))
    v_off = 2 * h_ * dk // dv

    def specs(mk):
        return [pl.BlockSpec((1, c, dk), mk(0)), pl.BlockSpec((1, c, dk), mk(h_)),
                pl.BlockSpec((1, c, dv), mk(v_off))]

    st = pl.BlockSpec((1, 1, 2, dk, dv), lambda bi, h, t: (bi, h, 0, 0, 0))
    return pl.pallas_call(
        _ret_core_kernel,
        grid=(b, h_, nt),
        in_specs=specs(fwd) + specs(bwd) + [pl.BlockSpec((1, 2, dv), lambda bi, h, t: (h, 0, 0)), st],
        out_specs=[pl.BlockSpec((1, c, dv), fwd(0)), pl.BlockSpec((1, c, dv), bwd(0)), st],
        out_shape=[jax.ShapeDtypeStruct((b, l, h_ * dv), BF16),
                   jax.ShapeDtypeStruct((b, l, h_ * dv), BF16),
                   jax.ShapeDtypeStruct((b, h_, 2, dk, dv), F32)],
        scratch_shapes=[pltpu.VMEM((2, dk, dv), F32), pltpu.VMEM((c, c), F32),
                        pltpu.VMEM((4, c, LANES), F32)],
        compiler_params=_cparams("parallel", "parallel", "arbitrary"),
        name="retention_core",
    )(qk, qk, pre, qk, qk, pre, lg, s0)


def _final_norm_kernel(x_ref, g_ref, o_ref):
    x = x_ref[0]
    o_ref[0] = x * lax.rsqrt(jnp.mean(x * x, axis=-1, keepdims=True) + EPS) * g_ref[...]


def _final_norm(x, g, tm=1024):
    b, l, d = x.shape
    tm = min(tm, l)
    return pl.pallas_call(
        _final_norm_kernel,
        grid=(b, l // tm),
        in_specs=[pl.BlockSpec((1, tm, d), lambda bi, i: (bi, i, 0)),
                  pl.BlockSpec((1, d), lambda bi, i: (0, 0))],
        out_specs=pl.BlockSpec((1, tm, d), lambda bi, i: (bi, i, 0)),
        out_shape=jax.ShapeDtypeStruct((b, l, d), F32),
        compiler_params=_cparams("parallel", "parallel"),
        name="final_norm",
    )(x, g.reshape(1, d))


def _attention_layer(x, xc, m, mc, norm_g, w_in, q_g, k_g, w_out, tables, want_ctx):
    hd = ATT_HEADS * ATT_HEAD_DIM
    kvd = ATT_KV_HEADS * ATT_HEAD_DIM
    w_in = w_in.astype(BF16)
    w_out = w_out.astype(BF16)
    gains = jnp.concatenate([jnp.tile(q_g * (ATT_HEAD_DIM ** -0.5 * LOG2E), (ATT_HEADS, 1)),
                             jnp.tile(k_g, (ATT_KV_HEADS, 1))], axis=0)
    n_heads = ATT_HEADS + ATT_KV_HEADS

    def project(t, mm, tabs):
        qkv = _proj(t, norm_g, mm[1], mm[0], w_in, BF16, tm=1024, tn=hd + 2 * kvd)
        return _head_prep(qkv, gains, tabs, norm=True), qkv[:, :, hd + kvd:]

    qk, v = project(x, m, tables)
    qkc, vc = project(xc, mc, None)
    kv_c = jnp.concatenate([qkc[:, :, hd:], vc], axis=-1)
    kv_all = jnp.concatenate([jnp.concatenate([qk[:, :, hd:], v], axis=-1), kv_c], axis=1)
    o = _attention(qk, kv_all)
    tm = min(512, x.shape[1])
    x = _out_res(_plain_prologue, [o], [_row_spec(tm, hd)], w_out, x, m[2], tm)
    if want_ctx:
        oc = _attention(qkc, kv_c)
        tmc = min(512, xc.shape[1])
        xc = _out_res(_plain_prologue, [oc], [_row_spec(tmc, hd)], w_out, xc, mc[2], tmc)
    return x, xc


def _gdn_layer(x, xc, m, mc, norm_g, w_in, conv_w, a_log, dt_bias, gn_g, w_out, want_ctx):
    hq, hv, dh = GDN_QK_HEADS, GDN_V_HEADS, GDN_HEAD_DIM
    cw = (2 * hq + hv) * dh
    zw = hv * dh
    w_main = w_in[:, :cw + zw].astype(BF16)
    w_ab = jnp.pad(w_in[:, cw + zw:], ((0, 0), (0, LANES - 4 * hv))).astype(BF16)
    w_out = w_out.astype(BF16)
    pad = (0, LANES - 2 * hv)
    alog_row = jnp.pad(a_log.reshape(-1), pad).reshape(1, LANES)
    dtb_row = jnp.pad(dt_bias.reshape(-1), pad).reshape(1, LANES)

    def project(t, mm):
        pre = _proj(t, norm_g, mm[1], mm[0], w_main, BF16, tm=1024, tn=2048)
        ab = _proj(t, norm_g, mm[1], mm[0], w_ab, F32, tm=1024, tn=LANES)
        return _gdn_conv(pre, conv_w, n_norm=2 * hq), pre, ab

    b = x.shape[0]
    qkv_c, pre_c, ab_c = project(xc, mc)
    qkv, pre, ab = project(x, m)
    s0 = jnp.zeros((b, 2, hv, dh, dh), F32)
    ocf, ocb, sc = _gdn_core(qkv_c, ab_c, alog_row, dtb_row, s0)
    of, ob, _ = _gdn_core(qkv, ab, alog_row, dtb_row, sc)

    def out(of_, ob_, pre_, t, mm):
        tm = min(512, t.shape[1])
        zspec = _row_spec(tm, zw, col=cw // zw)
        gspec = pl.BlockSpec((1, LANES), lambda bi, i: (0, 0))
        return _out_res(_gdn_prologue, [of_, ob_, pre_, gn_g.reshape(1, dh)],
                        [_row_spec(tm, zw), _row_spec(tm, zw), zspec, gspec], w_out, t, mm[2], tm)

    x = out(of, ob, pre, x, m)
    if want_ctx:
        xc = out(ocf, ocb, pre_c, xc, mc)
    return x, xc


def _retention_layer(x, xc, m, mc, norm_g, w_in, decay, w_out, tables, want_ctx):
    h_, dk, dv = RET_HEADS, RET_QK_DIM, RET_V_DIM
    qkw = 2 * h_ * dk
    w_in = w_in.astype(BF16)
    w_out = w_out.astype(BF16)
    gains = jnp.concatenate([jnp.full((h_, dk), dk ** -0.5, F32), jnp.ones((h_, dk), F32)], axis=0)
    lg = jnp.broadcast_to(-decay.astype(F32).T[:, :, None], (h_, 2, dv))

    def project(t, mm, tabs):
        pre = _proj(t, norm_g, mm[1], mm[0], w_in, BF16, tm=1024, tn=2048)
        return _head_prep(pre, gains, tabs, norm=False), pre

    b = x.shape[0]
    qk_c, vg_c = project(xc, mc, None)
    qk, vg = project(x, m, tables)
    s0 = jnp.zeros((b, h_, 2, dk, dv), F32)
    ocf, ocb, sc = _ret_core(qk_c, vg_c, lg, s0)
    of, ob, _ = _ret_core(qk, vg, lg, sc)
    vw = h_ * dv

    def out(of_, ob_, vg_, t, mm):
        tm = min(512, t.shape[1])
        return _out_res(_ret_prologue, [of_, ob_, vg_],
                        [_row_spec(tm, vw), _row_spec(tm, vw), _row_spec(tm, vw, col=2)],
                        w_out, t, mm[2], tm)

    x = out(of, ob, vg, x, m)
    if want_ctx:
        xc = out(ocf, ocb, vg_c, xc, mc)
    return x, xc


def _ffn(x, m, norm_g, w_in, conv_w, conv_b, w_out):
    f = w_out.shape[0]
    u = _proj(x, norm_g, m[4], m[3], w_in.astype(BF16), BF16, tm=1024, tn=f)
    return _ffn_out(u, conv_w, conv_b, w_out.astype(BF16), x, m[5])


def kernel(x, c, ctx, c_ctx, mod_w, mod_b, norm1_g, norm2_g, att_w_in, att_q_g, att_k_g, att_w_out, gdn_w_in, gdn_conv_w, gdn_A_log, gdn_dt_bias, gdn_norm_g, gdn_w_out, ret_w_in, ret_decay, ret_w_out, ffn_w_in, ffn_conv_w, ffn_conv_b, ffn_w_out, final_g):
    b, n, d = x.shape
    depth = mod_w.shape[0]
    cc = jnp.concatenate([c, c_ctx[None, :], jnp.zeros((8 - b - 1, d), F32)], axis=0)
    mods = _modulation(cc, mod_w, mod_b)
    tab_att = _rope_tables(n, ATT_HEAD_DIM)
    tab_ret = _rope_tables(n, RET_QK_DIM)
    xc = ctx
    for i in range(depth):
        last = i == depth - 1
        mi = mods[i].reshape(8, 6, d)
        m = [mi[:b, s][:, None, :] for s in range(6)]
        mc = [jnp.broadcast_to(mi[b, s][None, None, :], (b, 1, d)) for s in range(6)]
        kind, j = i % N_MIXERS, i // N_MIXERS
        if kind == 0:
            x, xc = _attention_layer(x, xc, m, mc, norm1_g[i], att_w_in[j], att_q_g[j], att_k_g[j],
                                     att_w_out[j], tab_att, not last)
        elif kind == 1:
            x, xc = _gdn_layer(x, xc, m, mc, norm1_g[i], gdn_w_in[j], gdn_conv_w[j], gdn_A_log[j],
                               gdn_dt_bias[j], gdn_norm_g[j], gdn_w_out[j], not last)
        else:
            x, xc = _retention_layer(x, xc, m, mc, norm1_g[i], ret_w_in[j], ret_decay[j],
                                     ret_w_out[j], tab_ret, not last)
        x = _ffn(x, m, norm2_g[i], ffn_w_in[i], ffn_conv_w[i], ffn_conv_b[i], ffn_w_out[i])
        if not last:
            xc = _ffn(xc, mc, norm2_g[i], ffn_w_in[i], ffn_conv_w[i], ffn_conv_b[i], ffn_w_out[i])
    return _final_norm(x, final_g)
```

```python
import functools

import jax
import jax.numpy as jnp
from jax import lax
from jax.experimental import pallas as pl
from jax.experimental.pallas import tpu as pltpu

F32 = jnp.float32
BF16 = jnp.bfloat16

EPS = 1e-6
LOG2E = 1.4426950408889634
ROPE_THETA = 10000.0
GRID_W = 64
N_MIXERS = 3

ATT_HEADS, ATT_KV_HEADS, ATT_HEAD_DIM = 8, 2, 128
GDN_QK_HEADS, GDN_V_HEADS, GDN_HEAD_DIM, GDN_CONV_W = 8, 16, 128, 5
RET_HEADS, RET_QK_DIM, RET_V_DIM = 8, 128, 256
FFN_CONV_W = 3

LANES = 128
BF16_ROWS = 16
MXU_DIM = 256
GDN_CHUNK = 128
GDN_INV_BLOCK = 16
GDN_HEADS_PER_STEP = 4
RET_CHUNK = 256
RET_HEADS_PER_STEP = 2
ATT_MAX_UNROLL = 11
VMEM_LIMIT = 56 * 1024 * 1024


def _cparams(*sem):
    return pltpu.CompilerParams(dimension_semantics=sem, vmem_limit_bytes=VMEM_LIMIT)


def _sigmoid(x):
    return 1.0 / (1.0 + jnp.exp(-x))


def _silu(x):
    return x * _sigmoid(x)


def _dot(a, b):
    return jnp.dot(a.astype(BF16), b.astype(BF16), preferred_element_type=F32)


def _dot_nt(a, b):
    return lax.dot_general(a.astype(BF16), b.astype(BF16), (((1,), (1,)), ((), ())),
                           preferred_element_type=F32)


def _mod_kernel(cc_ref, w_ref, b_ref, o_ref):
    s = _silu(cc_ref[...])
    o_ref[0] = _dot(s, w_ref[0]) + b_ref[0]


def _modulation(cc, mod_w, mod_b):
    depth, d, n6 = mod_w.shape
    tn = n6 // 4
    return pl.pallas_call(
        _mod_kernel,
        grid=(depth, n6 // tn),
        in_specs=[pl.BlockSpec((8, d), lambda l, j: (0, 0)),
                  pl.BlockSpec((1, d, tn), lambda l, j: (l, 0, j)),
                  pl.BlockSpec((1, 1, tn), lambda l, j: (l, 0, j))],
        out_specs=pl.BlockSpec((1, 8, tn), lambda l, j: (l, 0, j)),
        out_shape=jax.ShapeDtypeStruct((depth, 8, n6), F32),
        compiler_params=_cparams("parallel", "parallel"),
        name="modulation",
    )(cc, mod_w, mod_b.reshape(depth, 1, n6))


def _norm_mod(x, g, sc, sh):
    ms = jnp.mean(x * x, axis=-1, keepdims=True)
    return (x * lax.rsqrt(ms + EPS)) * g * (1.0 + sc) + sh


def _proj_kernel(x_ref, g_ref, sc_ref, sh_ref, w_ref, o_ref, h_ref):
    @pl.when(pl.program_id(2) == 0)
    def _():
        h_ref[...] = _norm_mod(x_ref[0], g_ref[...], sc_ref[0], sh_ref[0]).astype(BF16)

    o_ref[0] = jnp.dot(h_ref[...], w_ref[...], preferred_element_type=F32).astype(o_ref.dtype)


def _proj(x, g, sc, sh, w, out_dtype, tm, tn):
    b, l, d = x.shape
    n = w.shape[1]
    tm = min(tm, l)
    tn = min(tn, n)
    return pl.pallas_call(
        _proj_kernel,
        grid=(b, l // tm, n // tn),
        in_specs=[pl.BlockSpec((1, tm, d), lambda bi, i, j: (bi, i, 0)),
                  pl.BlockSpec((1, d), lambda bi, i, j: (0, 0)),
                  pl.BlockSpec((1, 1, d), lambda bi, i, j: (bi, 0, 0)),
                  pl.BlockSpec((1, 1, d), lambda bi, i, j: (bi, 0, 0)),
                  pl.BlockSpec((d, tn), lambda bi, i, j: (0, j))],
        out_specs=pl.BlockSpec((1, tm, tn), lambda bi, i, j: (bi, i, j)),
        out_shape=jax.ShapeDtypeStruct((b, l, n), out_dtype),
        scratch_shapes=[pltpu.VMEM((tm, d), BF16)],
        compiler_params=_cparams("parallel", "parallel", "arbitrary"),
        name="norm_mod_proj",
    )(x, g.reshape(1, d), sc, sh, w)


def _rope_tables(n, head_dim):
    rows = n // GRID_W
    row = jnp.repeat(jnp.arange(rows, dtype=F32), GRID_W)
    col = jnp.tile(jnp.arange(GRID_W, dtype=F32), rows)
    axis_dim = head_dim // 2
    inv_freq = ROPE_THETA ** (-jnp.arange(0, axis_dim, 2, dtype=F32) / axis_dim)
    ar, ac = row[:, None] * inv_freq, col[:, None] * inv_freq
    cr, sr, cc, sc = jnp.cos(ar), jnp.sin(ar), jnp.cos(ac), jnp.sin(ac)
    z = jnp.zeros_like(sr)
    return (jnp.concatenate([cr, cr, cc, cc], axis=-1),
            jnp.concatenate([-sr, z, -sc, z], axis=-1),
            jnp.concatenate([z, sr, z, sc], axis=-1))


def _head_prep_kernel(*refs, norm, rope):
    if rope:
        x_ref, g_ref, c_ref, sa_ref, sb_ref, o_ref = refs
    else:
        x_ref, g_ref, o_ref = refs
    for h in range(g_ref.shape[1]):
        sl = slice(h * LANES, (h + 1) * LANES)
        x = x_ref[0, :, sl].astype(F32)
        if norm:
            x = x * lax.rsqrt(jnp.mean(x * x, axis=-1, keepdims=True) + EPS)
        x = x * g_ref[0, h:h + 1, :]
        if rope:
            x = (x * c_ref[...] + pltpu.roll(x, 96, 1) * sa_ref[...]
                 + pltpu.roll(x, 32, 1) * sb_ref[...])
        o_ref[0, :, sl] = x.astype(o_ref.dtype)


def _head_prep(x, gains, tables, norm, first_head=0, tm=512):
    b, l, _ = x.shape
    nb = gains.shape[0]
    tm = min(tm, l)
    hw = max(h for h in (8, 5, 4, 2, 1) if nb % h == 0 and first_head % h == 0)
    c0 = first_head // hw
    in_specs = [pl.BlockSpec((1, tm, hw * LANES), lambda bi, i, c: (bi, i, c0 + c)),
                pl.BlockSpec((1, hw, LANES), lambda bi, i, c: (c, 0, 0))]
    args = [x, gains.reshape(nb // hw, hw, LANES)]
    if tables is not None:
        in_specs += [pl.BlockSpec((tm, LANES), lambda bi, i, c: (i, 0))] * 3
        args += list(tables)
    return pl.pallas_call(
        functools.partial(_head_prep_kernel, norm=norm, rope=tables is not None),
        grid=(b, l // tm, nb // hw),
        in_specs=in_specs,
        out_specs=pl.BlockSpec((1, tm, hw * LANES), lambda bi, i, c: (bi, i, c)),
        out_shape=jax.ShapeDtypeStruct((b, l, nb * LANES), BF16),
        compiler_params=_cparams("parallel", "parallel", "parallel"),
        name="head_prep",
    )(*args)


def _attn_kernel(*refs, groups, tq, unroll, rope):
    q_ref, g_ref = refs[:2]
    tabs = refs[2:5] if rope else None
    k_ref, vt_ref, o_ref = refs[5 if rope else 2:]
    dh = LANES
    qts = []
    for g in range(groups):
        x = q_ref[0, :, g * dh:(g + 1) * dh].astype(F32)
        x = x * lax.rsqrt(jnp.mean(x * x, axis=-1, keepdims=True) + EPS) * g_ref[0, g:g + 1, :]
        qts.append((_rotary(x, tabs) if rope else x).T)
    qt = jnp.concatenate(qts, axis=1).astype(BF16)
    rows = groups * tq
    nk, vrows, tk = vt_ref.shape[2:]

    def scores(j):
        off = pl.multiple_of(j * tk, tk)
        return jnp.dot(k_ref[0, pl.ds(off, tk), :], qt, preferred_element_type=F32)

    def body(it, carry):
        m, acc = carry
        base = it * unroll
        st = scores(base)
        pv = alpha = None
        for u in range(unroll):
            st_next = scores(base + u + 1) if u + 1 < unroll else None
            m_new = jnp.maximum(m, jnp.max(st, axis=0, keepdims=True))
            p = jnp.exp2(st - m_new).astype(BF16)
            if pv is not None:
                acc = alpha * acc + pv
            alpha = jnp.exp2(m - m_new)
            pv = jnp.dot(vt_ref[0, 0, base + u], p, preferred_element_type=F32)
            m, st = m_new, st_next
        return m, alpha * acc + pv

    init = (jnp.full((1, rows), -1e30, F32), jnp.zeros((vrows, rows), F32))
    _, acc = lax.fori_loop(0, nk // unroll, body, init)
    out = acc[:dh] / acc[dh:dh + 1]
    for g in range(groups):
        o_ref[0, :, g * dh:(g + 1) * dh] = out[:, g * tq:(g + 1) * tq].T.astype(o_ref.dtype)


def _attention(q, q_gains, tables, kv, tq=256, tk=256):
    b, n, _ = q.shape
    lk = kv.shape[1]
    kvh = ATT_KV_HEADS
    groups = ATT_HEADS // kvh
    hd = ATT_HEADS * LANES
    gw = groups * LANES
    tq = min(tq, n)
    tk = min(tk, lk)
    nk = lk // tk
    vt = kv[:, :, kvh * LANES:].reshape(b, nk, tk, kvh, LANES).transpose(0, 3, 1, 4, 2)
    vt = jnp.concatenate([vt, jnp.ones((b, kvh, nk, BF16_ROWS, tk), BF16)], axis=3)
    vrows = LANES + BF16_ROWS
    unroll = max(u for u in range(1, ATT_MAX_UNROLL + 1) if nk % u == 0)
    rope = tables is not None
    q_specs = [pl.BlockSpec((1, tq, gw), lambda bi, h, i: (bi, i, h)),
               pl.BlockSpec((1, groups, LANES), lambda bi, h, i: (h, 0, 0))]
    q_args = [q, q_gains.reshape(kvh, groups, LANES)]
    if rope:
        q_specs += [pl.BlockSpec((tq, LANES), lambda bi, h, i: (i, 0))] * 3
        q_args += list(tables)
    return pl.pallas_call(
        functools.partial(_attn_kernel, groups=groups, tq=tq, unroll=unroll, rope=rope),
        grid=(b, kvh, n // tq),
        in_specs=q_specs + [pl.BlockSpec((1, lk, LANES), lambda bi, h, i: (bi, 0, h)),
                            pl.BlockSpec((1, 1, nk, vrows, tk), lambda bi, h, i: (bi, h, 0, 0, 0))],
        out_specs=pl.BlockSpec((1, tq, gw), lambda bi, h, i: (bi, i, h)),
        out_shape=jax.ShapeDtypeStruct((b, n, hd), BF16),
        compiler_params=_cparams("parallel", "parallel", "parallel"),
        name="gqa_attention",
    )(*q_args, kv, vt)


def _plain_prologue(a_ref):
    return a_ref[0]


def _gdn_prologue(of_ref, ob_ref, z_ref, g_ref):
    outs = []
    g = g_ref[...]
    for h in range(GDN_V_HEADS):
        sl = slice(h * LANES, (h + 1) * LANES)
        o = of_ref[0, :, sl].astype(F32) + ob_ref[0, :, sl].astype(F32)
        o = o * lax.rsqrt(jnp.mean(o * o, axis=-1, keepdims=True) + EPS) * g
        outs.append((o * _silu(z_ref[0, :, sl].astype(F32))).astype(BF16))
    return jnp.concatenate(outs, axis=-1)


def _ret_prologue(of_ref, ob_ref, gate_ref):
    outs = []
    for h in range(RET_HEADS):
        sl = slice(h * RET_V_DIM, (h + 1) * RET_V_DIM)
        o = of_ref[0, :, sl].astype(F32) + ob_ref[0, :, sl].astype(F32)
        o = o * lax.rsqrt(jnp.mean(o * o, axis=-1, keepdims=True) + EPS)
        outs.append((o * _silu(gate_ref[0, :, sl].astype(F32))).astype(BF16))
    return jnp.concatenate(outs, axis=-1)


def _out_res_kernel(*refs, prologue, n_in):
    w_ref, x_ref, ga_ref, o_ref = refs[n_in:]
    a = prologue(*refs[:n_in])
    y = jnp.dot(a, w_ref[...], preferred_element_type=F32)
    o_ref[0] = x_ref[0] + ga_ref[0] * y


def _out_res(prologue, ins, in_specs, w, x, ga, tm):
    b, l, d = x.shape
    k = w.shape[0]
    return pl.pallas_call(
        functools.partial(_out_res_kernel, prologue=prologue, n_in=len(ins)),
        grid=(b, l // tm),
        in_specs=in_specs + [pl.BlockSpec((k, d), lambda bi, i: (0, 0)),
                             pl.BlockSpec((1, tm, d), lambda bi, i: (bi, i, 0)),
                             pl.BlockSpec((1, 1, d), lambda bi, i: (bi, 0, 0))],
        out_specs=pl.BlockSpec((1, tm, d), lambda bi, i: (bi, i, 0)),
        out_shape=jax.ShapeDtypeStruct((b, l, d), F32),
        compiler_params=_cparams("parallel", "parallel"),
        name="out_proj_residual",
    )(*ins, w, x, ga)


def _row_spec(tm, width, col=0):
    return pl.BlockSpec((1, tm, width), lambda bi, i: (bi, i, col))


def _convglu_kernel(x_ref, xp_ref, xn_ref, g_ref, sc_ref, sh_ref, win_ref, cw_ref, cb_ref, wout_ref,
                    ga_ref, fg_ref, o_ref, *, sub, final):
    i = pl.program_id(1)
    tm = x_ref.shape[1]
    f = wout_ref.shape[0]
    hl = xp_ref.shape[1]
    x = x_ref[0]
    xe = jnp.concatenate([xp_ref[0], x, xn_ref[0]], axis=0)
    he = _norm_mod(xe, g_ref[...], sc_ref[0], sh_ref[0]).astype(BF16)
    h = he[hl:hl + tm]
    r = lax.broadcasted_iota(jnp.int32, (tm + 2 * hl, 1), 0)
    inside = jnp.logical_and(jnp.logical_or(r >= hl, i > 0),
                             jnp.logical_or(r < hl + tm, i < pl.num_programs(1) - 1))
    keep = jnp.where(inside, 1.0, 0.0)

    def project(j):
        gate = jnp.dot(he, win_ref[:, f + j * sub:f + (j + 1) * sub],
                       preferred_element_type=F32) * keep
        val = jnp.dot(h, win_ref[:, j * sub:(j + 1) * sub], preferred_element_type=F32)
        return gate, val

    def act(gate, val, j):
        sl = slice(j * sub, (j + 1) * sub)
        cw = cw_ref[:, sl]
        conv = (cw[0:1] * gate[hl - 1:hl - 1 + tm] + cw[1:2] * gate[hl:hl + tm]
                + cw[2:3] * gate[hl + 1:hl + 1 + tm] + cb_ref[:, sl])
        gelu = 0.5 * conv * (1.0 + lax.erf(conv * (2.0 ** -0.5)))
        return (gelu * val).astype(BF16)

    n = f // sub
    gv, y = project(0), None
    for j in range(n):
        gv_next = project(j + 1) if j + 1 < n else None
        a = act(*gv, j)
        part = jnp.dot(a, wout_ref[j * sub:(j + 1) * sub, :], preferred_element_type=F32)
        y = part if y is None else y + part
        gv = gv_next
    out = x + ga_ref[0] * y
    if final:
        out = out * lax.rsqrt(jnp.mean(out * out, axis=-1, keepdims=True) + EPS) * fg_ref[...]
    o_ref[0] = out


def _convglu(x, g, sc, sh, w_in, conv_w, conv_b, w_out, ga, final_g=None, tm=512):
    b, l, d = x.shape
    f = w_out.shape[0]
    tm = min(tm, l)
    sub = MXU_DIM if f % MXU_DIM == 0 else LANES
    hl = BF16_ROWS
    hb = tm // hl
    last_hb = l // hl - 1
    final = final_g is not None
    fg = (final_g if final else jnp.ones((d,), F32)).reshape(1, d)
    const = lambda shape: pl.BlockSpec(shape, lambda bi, i: (0,) * len(shape))
    mod = pl.BlockSpec((1, 1, d), lambda bi, i: (bi, 0, 0))
    return pl.pallas_call(
        functools.partial(_convglu_kernel, sub=sub, final=final),
        grid=(b, l // tm),
        in_specs=[pl.BlockSpec((1, tm, d), lambda bi, i: (bi, i, 0)),
                  pl.BlockSpec((1, hl, d), lambda bi, i: (bi, jnp.maximum(i * hb - 1, 0), 0)),
                  pl.BlockSpec((1, hl, d), lambda bi, i: (bi, jnp.minimum((i + 1) * hb, last_hb), 0)),
                  const((1, d)), mod, mod, const((d, 2 * f)), const((FFN_CONV_W, f)),
                  const((1, f)), const((f, d)), mod, const((1, d))],
        out_specs=pl.BlockSpec((1, tm, d), lambda bi, i: (bi, i, 0)),
        out_shape=jax.ShapeDtypeStruct((b, l, d), F32),
        compiler_params=_cparams("parallel", "parallel"),
        name="convglu",
    )(x, x, x, g.reshape(1, d), sc, sh, w_in, conv_w, conv_b.reshape(1, f), w_out, ga, fg)


def _gdn_in_kernel(x_ref, xp_ref, xn_ref, g_ref, sc_ref, sh_ref, w_ref, wz_ref, wab_ref, cw_ref,
                   qkv_ref, z_ref, ab_ref, *, n_norm, sub):
    i = pl.program_id(1)
    tm = x_ref.shape[1]
    hl = xp_ref.shape[1]
    ch = cw_ref.shape[1]
    pad = GDN_CONV_W // 2
    xe = jnp.concatenate([xp_ref[0], x_ref[0], xn_ref[0]], axis=0)
    he = _norm_mod(xe, g_ref[...], sc_ref[0], sh_ref[0]).astype(BF16)
    h = he[hl:hl + tm]
    r = lax.broadcasted_iota(jnp.int32, (tm + 2 * hl, 1), 0)
    inside = jnp.logical_and(jnp.logical_or(r >= hl, i > 0),
                             jnp.logical_or(r < hl + tm, i < pl.num_programs(1) - 1))
    keep = jnp.where(inside, 1.0, 0.0)

    def project(j):
        return jnp.dot(he, w_ref[:, j * sub:(j + 1) * sub], preferred_element_type=F32) * keep

    def act(pre, j):
        for hh in range(sub // LANES):
            lo = j * sub + hh * LANES
            cw = cw_ref[:, lo:lo + LANES]
            src = pre[:, hh * LANES:(hh + 1) * LANES]
            y = cw[0:1] * src[hl - pad:hl - pad + tm]
            for t in range(1, GDN_CONV_W):
                y = y + cw[t:t + 1] * src[hl - pad + t:hl - pad + t + tm]
            y = _silu(y)
            if lo // LANES < n_norm:
                y = y * lax.rsqrt(jnp.sum(y * y, axis=-1, keepdims=True) + EPS)
            qkv_ref[0, :, lo:lo + LANES] = y.astype(qkv_ref.dtype)

    n = ch // sub
    pre = project(0)
    for j in range(n):
        pre_next = project(j + 1) if j + 1 < n else None
        if j == n - 1:
            z_ref[0] = jnp.dot(h, wz_ref[...], preferred_element_type=F32).astype(z_ref.dtype)
            ab_ref[0] = jnp.dot(h, wab_ref[...], preferred_element_type=F32)
        act(pre, j)
        pre = pre_next


def _gdn_in(x, g, sc, sh, w_qkv, w_z, w_ab, conv_w, n_norm, tm=512):
    b, l, d = x.shape
    ch, zw = w_qkv.shape[1], w_z.shape[1]
    tm = min(tm, l)
    hl = BF16_ROWS
    hb = tm // hl
    last_hb = l // hl - 1
    const = lambda shape: pl.BlockSpec(shape, lambda bi, i: (0,) * len(shape))
    mod = pl.BlockSpec((1, 1, d), lambda bi, i: (bi, 0, 0))
    rows = lambda w: pl.BlockSpec((1, tm, w), lambda bi, i: (bi, i, 0))
    return pl.pallas_call(
        functools.partial(_gdn_in_kernel, n_norm=n_norm, sub=MXU_DIM),
        grid=(b, l // tm),
        in_specs=[rows(d),
                  pl.BlockSpec((1, hl, d), lambda bi, i: (bi, jnp.maximum(i * hb - 1, 0), 0)),
                  pl.BlockSpec((1, hl, d), lambda bi, i: (bi, jnp.minimum((i + 1) * hb, last_hb), 0)),
                  const((1, d)), mod, mod, const((d, ch)), const((d, zw)), const((d, LANES)),
                  const((GDN_CONV_W, ch))],
        out_specs=[rows(ch), rows(zw), rows(LANES)],
        out_shape=[jax.ShapeDtypeStruct((b, l, ch), BF16), jax.ShapeDtypeStruct((b, l, zw), BF16),
                   jax.ShapeDtypeStruct((b, l, LANES), F32)],
        compiler_params=_cparams("parallel", "parallel"),
        name="gdn_in",
    )(x, x, x, g.reshape(1, d), sc, sh, w_qkv, w_z, w_ab, conv_w)


def _mm(xs, ys):
    return [jnp.dot(x, y, preferred_element_type=F32) for x, y in zip(xs, ys)]


def _bf(xs):
    return [x.astype(BF16) for x in xs]


def _neumann_minus_eye(ms, order):
    mb = _bf(ms)
    m2 = _mm(mb, mb)
    pb = _bf(m2)
    m3 = _mm(mb, pb)
    r = [b - a - c for a, b, c in zip(ms, m2, m3)]
    p, mp = 4, m2
    while p < order:
        mp = _mm(pb, pb)
        pb = _bf(mp)
        rm = _mm(_bf(r), pb)
        r = [x + y + z for x, y, z in zip(r, mp, rm)]
        p *= 2
    return r


def _unit_tri_inverse_minus_eye(a_list, same_block):
    c = a_list[0].shape[0]
    d = [jnp.where(same_block, a, 0.0) for a in a_list]
    lo = [a - x for a, x in zip(a_list, d)]
    r0 = _neumann_minus_eye(d, GDN_INV_BLOCK)
    r0b = _bf(r0)
    n = [x + y for x, y in zip(lo, _mm(r0b, _bf(lo)))]
    rn = _neumann_minus_eye(n, c // GDN_INV_BLOCK)
    return [x + y + z for x, y, z in zip(r0, rn, _mm(_bf(rn), r0b))]


def _gdn_core_kernel(qf_ref, kf_ref, vf_ref, abf_ref, qb_ref, kb_ref, vb_ref, abb_ref,
                     alog_ref, dtb_ref, s0_ref, of_ref, ob_ref, sout_ref, s_ref, gt_ref, *, hb):
    t = pl.program_id(2)
    grp = pl.program_id(1)
    rep = GDN_V_HEADS // GDN_QK_HEADS
    c, dk = GDN_CHUNK, GDN_HEAD_DIM
    scale = dk ** -0.5

    @pl.when(t == 0)
    def _():
        s_ref[...] = s0_ref[0]

    rr = lax.broadcasted_iota(jnp.int32, (c, c), 0)
    cc = lax.broadcasted_iota(jnp.int32, (c, c), 1)
    same_block = (rr // GDN_INV_BLOCK) == (cc // GDN_INV_BLOCK)
    lane = lax.broadcasted_iota(jnp.int32, (c, LANES), 1)
    ng = 2 * GDN_V_HEADS
    dirs = ((qf_ref, kf_ref, vf_ref, abf_ref, of_ref, rr >= cc, rr > cc),
            (qb_ref, kb_ref, vb_ref, abb_ref, ob_ref, rr <= cc, rr < cc))

    gcs, balls, kks, qks, kts = [], [], [], [], []
    for q_ref, k_ref, _, ab_ref, _, incl, _ in dirs:
        ab = ab_ref[0]
        sp = ab + dtb_ref[...]
        sp = jnp.maximum(sp, 0.0) + jnp.log1p(jnp.exp(-jnp.abs(sp)))
        gall = -jnp.exp(alog_ref[...]) * sp
        balls.append(_sigmoid(ab))
        g1 = gall.astype(BF16)
        r1 = gall - g1.astype(F32)
        g2 = r1.astype(BF16)
        g3 = (r1 - g2.astype(F32)).astype(BF16)
        packed = jnp.where(lane < ng, g1.astype(F32),
                           jnp.where(lane < 2 * ng, pltpu.roll(g2.astype(F32), ng, 1),
                                     jnp.where(lane < 3 * ng,
                                               pltpu.roll(g3.astype(F32), 2 * ng, 1), 0.0)))
        sel = jnp.concatenate([jnp.where(incl, 1.0, 0.0), jnp.ones((c, c), F32)], axis=0)
        cs = _dot(sel, packed)
        cs = cs + pltpu.roll(cs, LANES - ng, 1) + pltpu.roll(cs, LANES - 2 * ng, 1)
        gcs.append(cs[:c])
        d = len(gcs) - 1
        gt_ref[d, 0] = cs[:c].T
        gt_ref[d, 1] = cs[c:].T
        for hh in range(hb):
            sl = slice(hh * dk, (hh + 1) * dk)
            k = k_ref[0, :, sl]
            g = _dot_nt(jnp.concatenate([k, q_ref[0, :, sl]], axis=0), k)
            kks.append(g[:c])
            qks.append(g[c:])
            kts.append(k.astype(F32).T)

    chains = [(d, hh, e) for d in range(2) for hh in range(hb) for e in range(rep)]
    a_list, xs, qkms, qds, kdts, gls = [], [], [], [], [], []
    for d, hh, e in chains:
        q_ref, k_ref, v_ref, _, _, incl, strict = dirs[d]
        col = d * GDN_V_HEADS + (grp * hb + hh) * rep + e
        pick = lambda m, j: jnp.sum(jnp.where(lane == j, m, 0.0), axis=-1, keepdims=True)
        gc = pick(gcs[d], col)
        bcol = pick(balls[d], col + ng)
        gc_row = gt_ref[d, 0, pl.ds(col, 1), :]
        gtot_row = gt_ref[d, 1, pl.ds(col, 1), :]
        decay = jnp.where(incl, jnp.exp(jnp.minimum(gc - gc_row, 0.0)), 0.0)
        a_list.append(jnp.where(strict, bcol * kks[d * hb + hh] * decay, 0.0))
        kf = k_ref[0, :, hh * dk:(hh + 1) * dk].astype(F32)
        qf = q_ref[0, :, hh * dk:(hh + 1) * dk].astype(F32)
        vf = v_ref[0, :, (hh * rep + e) * dk:(hh * rep + e + 1) * dk].astype(F32)
        egc = jnp.exp(gc)
        xs.append(jnp.concatenate([vf * bcol, kf * (bcol * egc)], axis=1))
        qkms.append((qks[d * hb + hh] * decay * scale).astype(BF16))
        qds.append(qf * (egc * scale))
        kdts.append((kts[d * hb + hh] * jnp.exp(gtot_row - gc_row)).astype(BF16))
        gls.append(jnp.exp(gtot_row[:, :dk]))

    tms = _unit_tri_inverse_minus_eye(a_list, same_block)
    uws = [x + y for x, y in zip(xs, _mm(_bf(tms), _bf(xs)))]

    ss = [s_ref[i] for i in range(len(chains))]
    sbs = _bf(ss)
    wss = _mm([jnp.concatenate([uw[:, dk:], qd], axis=0).astype(BF16) for uw, qd in zip(uws, qds)],
              sbs)
    vns = [(uw[:, :dk] - ws[:c]).astype(BF16) for uw, ws in zip(uws, wss)]
    ovs = _mm(qkms, vns)
    kvs = _mm(kdts, vns)
    for i, (d, hh, e) in enumerate(chains):
        o_ref = dirs[d][4]
        o_ref[0, :, (hh * rep + e) * dk:(hh * rep + e + 1) * dk] = (
            wss[i][c:] + ovs[i]).astype(o_ref.dtype)
        s_ref[i] = ss[i] * gls[i] + kvs[i]

    @pl.when(t == pl.num_programs(2) - 1)
    def _():
        sout_ref[0] = s_ref[...]


def _gdn_core(qkv, ab, alog_row, dtb_row, s0, hb=GDN_HEADS_PER_STEP):
    b, l, _ = qkv.shape
    c = GDN_CHUNK
    nt = l // c
    hq, hv = GDN_QK_HEADS, GDN_V_HEADS
    rep = hv // hq
    ngrp = hq // hb
    fwd = lambda off: (lambda bi, g, t: (bi, t, off + g))
    bwd = lambda off: (lambda bi, g, t: (bi, nt - 1 - t, off + g))
    qw, vw = hb * LANES, hb * rep * LANES

    def specs(mk, ab_map):
        return [pl.BlockSpec((1, c, qw), mk(0)),
                pl.BlockSpec((1, c, qw), mk(ngrp)),
                pl.BlockSpec((1, c, vw), mk(ngrp)),
                pl.BlockSpec((1, c, LANES), ab_map)]

    ab_fwd = lambda bi, g, t: (bi, t, 0)
    ab_bwd = lambda bi, g, t: (bi, nt - 1 - t, 0)
    row = pl.BlockSpec((1, LANES), lambda bi, g, t: (0, 0))
    st = pl.BlockSpec((1, 2 * hb * rep, LANES, LANES), lambda bi, g, t: (bi * ngrp + g, 0, 0, 0))
    s0g = s0.reshape(b, 2, ngrp, hb * rep, LANES, LANES).transpose(0, 2, 1, 3, 4, 5)
    s0g = s0g.reshape(b * ngrp, 2 * hb * rep, LANES, LANES)
    of, ob, sg = pl.pallas_call(
        functools.partial(_gdn_core_kernel, hb=hb),
        grid=(b, ngrp, nt),
        in_specs=specs(fwd, ab_fwd) + specs(bwd, ab_bwd) + [row, row, st],
        out_specs=[pl.BlockSpec((1, c, vw), fwd(0)), pl.BlockSpec((1, c, vw), bwd(0)), st],
        out_shape=[jax.ShapeDtypeStruct((b, l, hv * LANES), BF16),
                   jax.ShapeDtypeStruct((b, l, hv * LANES), BF16),
                   jax.ShapeDtypeStruct(s0g.shape, F32)],
        scratch_shapes=[pltpu.VMEM((2 * hb * rep, LANES, LANES), F32),
                        pltpu.VMEM((2, 2, LANES, c), F32)],
        compiler_params=_cparams("parallel", "parallel", "arbitrary"),
        name="gdn_core",
    )(qkv, qkv, qkv, ab, qkv, qkv, qkv, ab, alog_row, dtb_row, s0g)
    sg = sg.reshape(b, ngrp, 2, hb * rep, LANES, LANES).transpose(0, 2, 1, 3, 4, 5)
    return of, ob, sg.reshape(b, 2, hv, LANES, LANES)


def _rotary(x, tabs):
    c_ref, sa_ref, sb_ref = tabs
    return (x * c_ref[...] + pltpu.roll(x, 96, 1) * sa_ref[...]
            + pltpu.roll(x, 32, 1) * sb_ref[...])


def _ret_core_kernel(*refs, hp, rope):
    qf_ref, kf_ref, vf_ref, qb_ref, kb_ref, vb_ref = refs[:6]
    tabs_f, tabs_b = (refs[6:9], refs[9:12]) if rope else (None, None)
    lg_ref, s0_ref, of_ref, ob_ref, sout_ref, s_ref, mask_ref, dec_ref = refs[12 if rope else 6:]
    t = pl.program_id(2)
    c = qf_ref.shape[1]
    dk, dv = RET_QK_DIM, RET_V_DIM
    scale = dk ** -0.5

    @pl.when(t == 0)
    def _():
        rr = lax.broadcasted_iota(jnp.int32, (c, c), 0)
        cc = lax.broadcasted_iota(jnp.int32, (c, c), 1)
        diff = (rr - cc).astype(F32)
        pos = lax.broadcasted_iota(jnp.int32, (c, LANES), 0).astype(F32)
        for h in range(hp):
            lgf, lgb = lg_ref[h, 0:1, :], lg_ref[h, 1:2, :]
            s_ref[2 * h] = s0_ref[0, h, 0]
            s_ref[2 * h + 1] = s0_ref[0, h, 1]
            mask_ref[h] = scale * jnp.where(diff > 0, jnp.exp(diff * lgf[:, :c]),
                                            jnp.where(diff < 0, jnp.exp(-diff * lgb[:, :c]), 2.0))
            lgf1, lgb1 = lgf[:, :LANES], lgb[:, :LANES]
            dec_ref[4 * h] = scale * jnp.exp((pos + 1.0) * lgf1)
            dec_ref[4 * h + 1] = jnp.exp((c - 1.0 - pos) * lgf1)
            dec_ref[4 * h + 2] = scale * jnp.exp((c - pos) * lgb1)
            dec_ref[4 * h + 3] = jnp.exp(pos * lgb1)

    def load(ref, h, tabs):
        x = ref[0, :, h * dk:(h + 1) * dk].astype(F32)
        return _rotary(x, tabs) if rope else x

    heads = []
    for h in range(hp):
        q, k = load(qf_ref, h, tabs_f), load(kf_ref, h, tabs_f)
        qb, kb = load(qb_ref, h, tabs_b), load(kb_ref, h, tabs_b)
        v, vb = vf_ref[0, :, h * dv:(h + 1) * dv], vb_ref[0, :, h * dv:(h + 1) * dv]
        sf, sb = s_ref[2 * h], s_ref[2 * h + 1]
        heads.append((v, sf, sb,
                      _dot_nt(q, k),
                      _dot(q * dec_ref[4 * h], sf),
                      _dot((k * dec_ref[4 * h + 1]).T, v),
                      _dot(qb * dec_ref[4 * h + 2], sb),
                      _dot((kb * dec_ref[4 * h + 3]).T, vb)))
    for h, (v, sf, sb, sc, inter_f, kv_f, inter_b, kv_b) in enumerate(heads):
        sl = slice(h * dv, (h + 1) * dv)
        of_ref[0, :, sl] = (_dot(sc * mask_ref[h], v) + inter_f).astype(of_ref.dtype)
        ob_ref[0, :, sl] = inter_b.astype(ob_ref.dtype)
        s_ref[2 * h] = sf * jnp.exp(c * lg_ref[h, 0:1, :]) + kv_f
        s_ref[2 * h + 1] = sb * jnp.exp(c * lg_ref[h, 1:2, :]) + kv_b

    @pl.when(t == pl.num_programs(2) - 1)
    def _():
        for h in range(hp):
            sout_ref[0, h, 0] = s_ref[2 * h]
            sout_ref[0, h, 1] = s_ref[2 * h + 1]


def _ret_core(pre, tables, lg, s0, hp=RET_HEADS_PER_STEP):
    b, l, _ = pre.shape
    c = min(RET_CHUNK, l)
    nt = l // c
    h_, dk, dv = RET_HEADS, RET_QK_DIM, RET_V_DIM
    ng = h_ // hp
    fwd = lambda off: (lambda bi, g, t: (bi, t, off + g))
    bwd = lambda off: (lambda bi, g, t: (bi, nt - 1 - t, off + g))

    def specs(mk):
        return [pl.BlockSpec((1, c, hp * dk), mk(0)), pl.BlockSpec((1, c, hp * dk), mk(ng)),
                pl.BlockSpec((1, c, hp * dv), mk(ng))]

    rope = tables is not None
    tab_specs, tab_args = [], []
    if rope:
        tab_specs = ([pl.BlockSpec((c, LANES), lambda bi, g, t: (t, 0))] * 3
                     + [pl.BlockSpec((c, LANES), lambda bi, g, t: (nt - 1 - t, 0))] * 3)
        tab_args = list(tables) * 2
    st = pl.BlockSpec((1, hp, 2, dk, dv), lambda bi, g, t: (bi, g, 0, 0, 0))
    return pl.pallas_call(
        functools.partial(_ret_core_kernel, hp=hp, rope=rope),
        grid=(b, ng, nt),
        in_specs=(specs(fwd) + specs(bwd) + tab_specs
                  + [pl.BlockSpec((hp, 2, dv), lambda bi, g, t: (g, 0, 0)), st]),
        out_specs=[pl.BlockSpec((1, c, hp * dv), fwd(0)), pl.BlockSpec((1, c, hp * dv), bwd(0)), st],
        out_shape=[jax.ShapeDtypeStruct((b, l, h_ * dv), BF16),
                   jax.ShapeDtypeStruct((b, l, h_ * dv), BF16),
                   jax.ShapeDtypeStruct((b, h_, 2, dk, dv), F32)],
        scratch_shapes=[pltpu.VMEM((2 * hp, dk, dv), F32), pltpu.VMEM((hp, c, c), F32),
                        pltpu.VMEM((4 * hp, c, LANES), F32)],
        compiler_params=_cparams("parallel", "parallel", "arbitrary"),
        name="retention_core",
    )(pre, pre, pre, pre, pre, pre, *tab_args, lg, s0)


def _attention_layer(x, xc, m, mc, norm_g, w_in, q_g, k_g, w_out, tables, want_ctx):
    hd = ATT_HEADS * ATT_HEAD_DIM
    kvd = ATT_KV_HEADS * ATT_HEAD_DIM
    w_in = w_in.astype(BF16)
    w_out = w_out.astype(BF16)
    q_gains = jnp.tile(q_g * (ATT_HEAD_DIM ** -0.5 * LOG2E), (ATT_HEADS, 1))
    k_gains = jnp.tile(k_g, (ATT_KV_HEADS, 1))

    def project(t, mm, tabs):
        qkv = _proj(t, norm_g, mm[1], mm[0], w_in, BF16, tm=1024, tn=hd + 2 * kvd)
        k = _head_prep(qkv, k_gains, tabs, norm=True, first_head=ATT_HEADS)
        return qkv, jnp.concatenate([k, qkv[:, :, hd + kvd:]], axis=-1)

    qkv, kv = project(x, m, tables)
    qkv_c, kv_c = project(xc, mc, None)
    o = _attention(qkv, q_gains, tables, jnp.concatenate([kv, kv_c], axis=1))
    tm = min(512, x.shape[1])
    x = _out_res(_plain_prologue, [o], [_row_spec(tm, hd)], w_out, x, m[2], tm)
    if want_ctx:
        oc = _attention(qkv_c, q_gains, None, kv_c)
        tmc = min(512, xc.shape[1])
        xc = _out_res(_plain_prologue, [oc], [_row_spec(tmc, hd)], w_out, xc, mc[2], tmc)
    return x, xc


def _gdn_layer(x, xc, m, mc, norm_g, w_in, conv_w, a_log, dt_bias, gn_g, w_out, want_ctx):
    hq, hv, dh = GDN_QK_HEADS, GDN_V_HEADS, GDN_HEAD_DIM
    cw = (2 * hq + hv) * dh
    zw = hv * dh
    w_qkv = w_in[:, :cw].astype(BF16)
    w_z = w_in[:, cw:cw + zw].astype(BF16)
    w_ab = jnp.pad(w_in[:, cw + zw:], ((0, 0), (0, LANES - 4 * hv))).astype(BF16)
    w_out = w_out.astype(BF16)
    pad = (0, LANES - 2 * hv)
    alog_row = jnp.pad(a_log.reshape(-1), pad).reshape(1, LANES)
    dtb_row = jnp.pad(dt_bias.reshape(-1), pad).reshape(1, LANES)

    def project(t, mm):
        return _gdn_in(t, norm_g, mm[1], mm[0], w_qkv, w_z, w_ab, conv_w, n_norm=2 * hq)

    b = x.shape[0]
    qkv_c, z_c, ab_c = project(xc, mc)
    qkv, z, ab = project(x, m)
    s0 = jnp.zeros((b, 2, hv, dh, dh), F32)
    ocf, ocb, sc = _gdn_core(qkv_c, ab_c, alog_row, dtb_row, s0)
    of, ob, _ = _gdn_core(qkv, ab, alog_row, dtb_row, sc)

    def out(of_, ob_, z_, t, mm):
        tm = min(512, t.shape[1])
        gspec = pl.BlockSpec((1, LANES), lambda bi, i: (0, 0))
        return _out_res(_gdn_prologue, [of_, ob_, z_, gn_g.reshape(1, dh)],
                        [_row_spec(tm, zw)] * 3 + [gspec], w_out, t, mm[2], tm)

    x = out(of, ob, z, x, m)
    if want_ctx:
        xc = out(ocf, ocb, z_c, xc, mc)
    return x, xc


def _retention_layer(x, xc, m, mc, norm_g, w_in, decay, w_out, tables, want_ctx):
    h_, dk, dv = RET_HEADS, RET_QK_DIM, RET_V_DIM
    w_in = w_in.astype(BF16)
    w_out = w_out.astype(BF16)
    lg = jnp.broadcast_to(-decay.astype(F32).T[:, :, None], (h_, 2, dv))

    def project(t, mm):
        return _proj(t, norm_g, mm[1], mm[0], w_in, BF16, tm=1024, tn=2048)

    b = x.shape[0]
    vg_c = project(xc, mc)
    vg = project(x, m)
    s0 = jnp.zeros((b, h_, 2, dk, dv), F32)
    ocf, ocb, sc = _ret_core(vg_c, None, lg, s0)
    of, ob, _ = _ret_core(vg, tables, lg, sc)
    vw = h_ * dv

    def out(of_, ob_, vg_, t, mm):
        tm = min(512, t.shape[1])
        return _out_res(_ret_prologue, [of_, ob_, vg_],
                        [_row_spec(tm, vw), _row_spec(tm, vw), _row_spec(tm, vw, col=2)],
                        w_out, t, mm[2], tm)

    x = out(of, ob, vg, x, m)
    if want_ctx:
        xc = out(ocf, ocb, vg_c, xc, mc)
    return x, xc


def _ffn(x, m, norm_g, w_in, conv_w, conv_b, w_out, final_g=None):
    return _convglu(x, norm_g, m[4], m[3], w_in.astype(BF16), conv_w, conv_b, w_out.astype(BF16),
                    m[5], final_g)


def kernel(x, c, ctx, c_ctx, mod_w, mod_b, norm1_g, norm2_g, att_w_in, att_q_g, att_k_g, att_w_out, gdn_w_in, gdn_conv_w, gdn_A_log, gdn_dt_bias, gdn_norm_g, gdn_w_out, ret_w_in, ret_decay, ret_w_out, ffn_w_in, ffn_conv_w, ffn_conv_b, ffn_w_out, final_g):
    b, n, d = x.shape
    depth = mod_w.shape[0]
    cc = jnp.concatenate([c, c_ctx[None, :], jnp.zeros((8 - b - 1, d), F32)], axis=0)
    mods = _modulation(cc, mod_w, mod_b)
    tab_att = _rope_tables(n, ATT_HEAD_DIM)
    tab_ret = _rope_tables(n, RET_QK_DIM)
    xc = ctx
    for i in range(depth):
        last = i == depth - 1
        mi = mods[i].reshape(8, 6, d)
        m = [mi[:b, s][:, None, :] for s in range(6)]
        mc = [jnp.broadcast_to(mi[b, s][None, None, :], (b, 1, d)) for s in range(6)]
        kind, j = i % N_MIXERS, i // N_MIXERS
        if kind == 0:
            x, xc = _attention_layer(x, xc, m, mc, norm1_g[i], att_w_in[j], att_q_g[j], att_k_g[j],
                                     att_w_out[j], tab_att, not last)
        elif kind == 1:
            x, xc = _gdn_layer(x, xc, m, mc, norm1_g[i], gdn_w_in[j], gdn_conv_w[j], gdn_A_log[j],
                               gdn_dt_bias[j], gdn_norm_g[j], gdn_w_out[j], not last)
        else:
            x, xc = _retention_layer(x, xc, m, mc, norm1_g[i], ret_w_in[j], ret_decay[j],
                                     ret_w_out[j], tab_ret, not last)
        x = _ffn(x, m, norm2_g[i], ffn_w_in[i], ffn_conv_w[i], ffn_conv_b[i], ffn_w_out[i],
                 final_g if last else None)
        if not last:
            xc = _ffn(xc, mc, norm2_g[i], ffn_w_in[i], ffn_conv_w[i], ffn_conv_b[i], ffn_w_out[i])
    return x
```

```python
import functools

import jax
import jax.numpy as jnp
from jax import lax
from jax.experimental import pallas as pl
from jax.experimental.pallas import tpu as pltpu

F32 = jnp.float32
BF16 = jnp.bfloat16

EPS = 1e-6
LOG2E = 1.4426950408889634
ROPE_THETA = 10000.0
GRID_W = 64
N_MIXERS = 3

ATT_HEADS, ATT_KV_HEADS, ATT_HEAD_DIM = 8, 2, 128
GDN_QK_HEADS, GDN_V_HEADS, GDN_HEAD_DIM, GDN_CONV_W = 8, 16, 128, 5
RET_HEADS, RET_QK_DIM, RET_V_DIM = 8, 128, 256
FFN_CONV_W = 3

LANES = 128
BF16_ROWS = 16
MXU_DIM = 256
GDN_CHUNK = 128
GDN_INV_BLOCK = 16
GDN_HEADS_PER_STEP = 4
RET_CHUNK = 256
RET_HEADS_PER_STEP = 2
ATT_MAX_UNROLL = 11
VMEM_LIMIT = 56 * 1024 * 1024


def _cparams(*sem):
    return pltpu.CompilerParams(dimension_semantics=sem, vmem_limit_bytes=VMEM_LIMIT)


def _sigmoid(x):
    return 1.0 / (1.0 + jnp.exp(-x))


def _silu(x):
    return x * _sigmoid(x)


def _dot(a, b):
    return jnp.dot(a.astype(BF16), b.astype(BF16), preferred_element_type=F32)


def _dot_nt(a, b):
    return lax.dot_general(a.astype(BF16), b.astype(BF16), (((1,), (1,)), ((), ())),
                           preferred_element_type=F32)


def _mod_kernel(cc_ref, w_ref, b_ref, o_ref):
    s = _silu(cc_ref[...])
    o_ref[0] = _dot(s, w_ref[0]) + b_ref[0]


def _modulation(cc, mod_w, mod_b):
    depth, d, n6 = mod_w.shape
    tn = n6 // 4
    return pl.pallas_call(
        _mod_kernel,
        grid=(depth, n6 // tn),
        in_specs=[pl.BlockSpec((8, d), lambda l, j: (0, 0)),
                  pl.BlockSpec((1, d, tn), lambda l, j: (l, 0, j)),
                  pl.BlockSpec((1, 1, tn), lambda l, j: (l, 0, j))],
        out_specs=pl.BlockSpec((1, 8, tn), lambda l, j: (l, 0, j)),
        out_shape=jax.ShapeDtypeStruct((depth, 8, n6), F32),
        compiler_params=_cparams("parallel", "parallel"),
        name="modulation",
    )(cc, mod_w, mod_b.reshape(depth, 1, n6))


def _norm_mod(x, g, sc, sh):
    ms = jnp.mean(x * x, axis=-1, keepdims=True)
    return (x * lax.rsqrt(ms + EPS)) * g * (1.0 + sc) + sh


def _proj_kernel(x_ref, g_ref, sc_ref, sh_ref, w_ref, o_ref, h_ref):
    @pl.when(pl.program_id(2) == 0)
    def _():
        h_ref[...] = _norm_mod(x_ref[0], g_ref[...], sc_ref[0], sh_ref[0]).astype(BF16)

    o_ref[0] = jnp.dot(h_ref[...], w_ref[...], preferred_element_type=F32).astype(o_ref.dtype)


def _proj(x, g, sc, sh, w, out_dtype, tm, tn):
    b, l, d = x.shape
    n = w.shape[1]
    tm = min(tm, l)
    tn = min(tn, n)
    return pl.pallas_call(
        _proj_kernel,
        grid=(b, l // tm, n // tn),
        in_specs=[pl.BlockSpec((1, tm, d), lambda bi, i, j: (bi, i, 0)),
                  pl.BlockSpec((1, d), lambda bi, i, j: (0, 0)),
                  pl.BlockSpec((1, 1, d), lambda bi, i, j: (bi, 0, 0)),
                  pl.BlockSpec((1, 1, d), lambda bi, i, j: (bi, 0, 0)),
                  pl.BlockSpec((d, tn), lambda bi, i, j: (0, j))],
        out_specs=pl.BlockSpec((1, tm, tn), lambda bi, i, j: (bi, i, j)),
        out_shape=jax.ShapeDtypeStruct((b, l, n), out_dtype),
        scratch_shapes=[pltpu.VMEM((tm, d), BF16)],
        compiler_params=_cparams("parallel", "parallel", "arbitrary"),
        name="norm_mod_proj",
    )(x, g.reshape(1, d), sc, sh, w)


def _rope_tables(n, head_dim):
    rows = n // GRID_W
    row = jnp.repeat(jnp.arange(rows, dtype=F32), GRID_W)
    col = jnp.tile(jnp.arange(GRID_W, dtype=F32), rows)
    axis_dim = head_dim // 2
    inv_freq = ROPE_THETA ** (-jnp.arange(0, axis_dim, 2, dtype=F32) / axis_dim)
    ar, ac = row[:, None] * inv_freq, col[:, None] * inv_freq
    cr, sr, cc, sc = jnp.cos(ar), jnp.sin(ar), jnp.cos(ac), jnp.sin(ac)
    z = jnp.zeros_like(sr)
    return (jnp.concatenate([cr, cr, cc, cc], axis=-1),
            jnp.concatenate([-sr, z, -sc, z], axis=-1),
            jnp.concatenate([z, sr, z, sc], axis=-1))


def _head_prep_kernel(*refs, norm, rope):
    if rope:
        x_ref, g_ref, c_ref, sa_ref, sb_ref, o_ref = refs
    else:
        x_ref, g_ref, o_ref = refs
    for h in range(g_ref.shape[1]):
        sl = slice(h * LANES, (h + 1) * LANES)
        x = x_ref[0, :, sl].astype(F32)
        if norm:
            x = x * lax.rsqrt(jnp.mean(x * x, axis=-1, keepdims=True) + EPS)
        x = x * g_ref[0, h:h + 1, :]
        if rope:
            x = (x * c_ref[...] + pltpu.roll(x, 96, 1) * sa_ref[...]
                 + pltpu.roll(x, 32, 1) * sb_ref[...])
        o_ref[0, :, sl] = x.astype(o_ref.dtype)


def _head_prep(x, gains, tables, norm, tm=512):
    b, l, _ = x.shape
    nb = gains.shape[0]
    tm = min(tm, l)
    hw = max(h for h in (8, 5, 4, 2, 1) if nb % h == 0)
    in_specs = [pl.BlockSpec((1, tm, hw * LANES), lambda bi, i, c: (bi, i, c)),
                pl.BlockSpec((1, hw, LANES), lambda bi, i, c: (c, 0, 0))]
    args = [x, gains.reshape(nb // hw, hw, LANES)]
    if tables is not None:
        in_specs += [pl.BlockSpec((tm, LANES), lambda bi, i, c: (i, 0))] * 3
        args += list(tables)
    return pl.pallas_call(
        functools.partial(_head_prep_kernel, norm=norm, rope=tables is not None),
        grid=(b, l // tm, nb // hw),
        in_specs=in_specs,
        out_specs=pl.BlockSpec((1, tm, hw * LANES), lambda bi, i, c: (bi, i, c)),
        out_shape=jax.ShapeDtypeStruct((b, l, nb * LANES), BF16),
        compiler_params=_cparams("parallel", "parallel", "parallel"),
        name="head_prep",
    )(*args)


def _attn_kernel(q_ref, k_ref, vt_ref, o_ref, *, groups, tq, unroll):
    dh = LANES
    qt = jnp.concatenate([q_ref[0, :, g * dh:(g + 1) * dh].astype(F32).T for g in range(groups)],
                         axis=1).astype(BF16)
    rows = groups * tq
    nk, vrows, tk = vt_ref.shape[2:]

    def scores(j):
        off = pl.multiple_of(j * tk, tk)
        return jnp.dot(k_ref[0, pl.ds(off, tk), :], qt, preferred_element_type=F32)

    def body(it, carry):
        m, acc = carry
        base = it * unroll
        st = scores(base)
        pv = alpha = None
        for u in range(unroll):
            st_next = scores(base + u + 1) if u + 1 < unroll else None
            m_new = jnp.maximum(m, jnp.max(st, axis=0, keepdims=True))
            p = jnp.exp2(st - m_new).astype(BF16)
            if pv is not None:
                acc = alpha * acc + pv
            alpha = jnp.exp2(m - m_new)
            pv = jnp.dot(vt_ref[0, 0, base + u], p, preferred_element_type=F32)
            m, st = m_new, st_next
        return m, alpha * acc + pv

    init = (jnp.full((1, rows), -1e30, F32), jnp.zeros((vrows, rows), F32))
    _, acc = lax.fori_loop(0, nk // unroll, body, init)
    out = acc[:dh] / acc[dh:dh + 1]
    for g in range(groups):
        o_ref[0, :, g * dh:(g + 1) * dh] = out[:, g * tq:(g + 1) * tq].T.astype(o_ref.dtype)


def _attention(q, kv, tq=512, tk=256):
    b, n, _ = q.shape
    lk = kv.shape[1]
    kvh = ATT_KV_HEADS
    groups = ATT_HEADS // kvh
    hd = ATT_HEADS * LANES
    gw = groups * LANES
    tq = min(tq, n)
    tk = min(tk, lk)
    nk = lk // tk
    vt = kv[:, :, kvh * LANES:].reshape(b, nk, tk, kvh, LANES).transpose(0, 3, 1, 4, 2)
    vt = jnp.concatenate([vt, jnp.ones((b, kvh, nk, BF16_ROWS, tk), BF16)], axis=3)
    vrows = LANES + BF16_ROWS
    unroll = max(u for u in range(1, ATT_MAX_UNROLL + 1) if nk % u == 0)
    return pl.pallas_call(
        functools.partial(_attn_kernel, groups=groups, tq=tq, unroll=unroll),
        grid=(b, kvh, n // tq),
        in_specs=[pl.BlockSpec((1, tq, gw), lambda bi, h, i: (bi, i, h)),
                  pl.BlockSpec((1, lk, LANES), lambda bi, h, i: (bi, 0, h)),
                  pl.BlockSpec((1, 1, nk, vrows, tk), lambda bi, h, i: (bi, h, 0, 0, 0))],
        out_specs=pl.BlockSpec((1, tq, gw), lambda bi, h, i: (bi, i, h)),
        out_shape=jax.ShapeDtypeStruct((b, n, hd), BF16),
        compiler_params=_cparams("parallel", "parallel", "parallel"),
        name="gqa_attention",
    )(q, kv, vt)


def _plain_prologue(a_ref):
    return a_ref[0]


def _gdn_prologue(of_ref, ob_ref, z_ref, g_ref):
    outs = []
    g = g_ref[...]
    for h in range(GDN_V_HEADS):
        sl = slice(h * LANES, (h + 1) * LANES)
        o = of_ref[0, :, sl].astype(F32) + ob_ref[0, :, sl].astype(F32)
        o = o * lax.rsqrt(jnp.mean(o * o, axis=-1, keepdims=True) + EPS) * g
        outs.append((o * _silu(z_ref[0, :, sl].astype(F32))).astype(BF16))
    return jnp.concatenate(outs, axis=-1)


def _ret_prologue(of_ref, ob_ref, gate_ref):
    outs = []
    for h in range(RET_HEADS):
        sl = slice(h * RET_V_DIM, (h + 1) * RET_V_DIM)
        o = of_ref[0, :, sl].astype(F32) + ob_ref[0, :, sl].astype(F32)
        o = o * lax.rsqrt(jnp.mean(o * o, axis=-1, keepdims=True) + EPS)
        outs.append((o * _silu(gate_ref[0, :, sl].astype(F32))).astype(BF16))
    return jnp.concatenate(outs, axis=-1)


def _out_res_kernel(*refs, prologue, n_in):
    w_ref, x_ref, ga_ref, o_ref = refs[n_in:]
    a = prologue(*refs[:n_in])
    y = jnp.dot(a, w_ref[...], preferred_element_type=F32)
    o_ref[0] = x_ref[0] + ga_ref[0] * y


def _out_res(prologue, ins, in_specs, w, x, ga, tm):
    b, l, d = x.shape
    k = w.shape[0]
    return pl.pallas_call(
        functools.partial(_out_res_kernel, prologue=prologue, n_in=len(ins)),
        grid=(b, l // tm),
        in_specs=in_specs + [pl.BlockSpec((k, d), lambda bi, i: (0, 0)),
                             pl.BlockSpec((1, tm, d), lambda bi, i: (bi, i, 0)),
                             pl.BlockSpec((1, 1, d), lambda bi, i: (bi, 0, 0))],
        out_specs=pl.BlockSpec((1, tm, d), lambda bi, i: (bi, i, 0)),
        out_shape=jax.ShapeDtypeStruct((b, l, d), F32),
        compiler_params=_cparams("parallel", "parallel"),
        name="out_proj_residual",
    )(*ins, w, x, ga)


def _row_spec(tm, width, col=0):
    return pl.BlockSpec((1, tm, width), lambda bi, i: (bi, i, col))


def _convglu_kernel(x_ref, xp_ref, xn_ref, g_ref, sc_ref, sh_ref, win_ref, cw_ref, cb_ref, wout_ref,
                    ga_ref, fg_ref, o_ref, *, sub, final):
    i = pl.program_id(1)
    tm = x_ref.shape[1]
    f = wout_ref.shape[0]
    hl = xp_ref.shape[1]
    x = x_ref[0]
    xe = jnp.concatenate([xp_ref[0], x, xn_ref[0]], axis=0)
    he = _norm_mod(xe, g_ref[...], sc_ref[0], sh_ref[0]).astype(BF16)
    h = he[hl:hl + tm]
    r = lax.broadcasted_iota(jnp.int32, (tm + 2 * hl, 1), 0)
    inside = jnp.logical_and(jnp.logical_or(r >= hl, i > 0),
                             jnp.logical_or(r < hl + tm, i < pl.num_programs(1) - 1))
    keep = jnp.where(inside, 1.0, 0.0)

    def project(j):
        gate = jnp.dot(he, win_ref[:, f + j * sub:f + (j + 1) * sub],
                       preferred_element_type=F32) * keep
        val = jnp.dot(h, win_ref[:, j * sub:(j + 1) * sub], preferred_element_type=F32)
        return gate, val

    def act(gate, val, j):
        sl = slice(j * sub, (j + 1) * sub)
        cw = cw_ref[:, sl]
        conv = (cw[0:1] * gate[hl - 1:hl - 1 + tm] + cw[1:2] * gate[hl:hl + tm]
                + cw[2:3] * gate[hl + 1:hl + 1 + tm] + cb_ref[:, sl])
        gelu = 0.5 * conv * (1.0 + lax.erf(conv * (2.0 ** -0.5)))
        return (gelu * val).astype(BF16)

    n = f // sub
    gv, y = project(0), None
    for j in range(n):
        gv_next = project(j + 1) if j + 1 < n else None
        a = act(*gv, j)
        part = jnp.dot(a, wout_ref[j * sub:(j + 1) * sub, :], preferred_element_type=F32)
        y = part if y is None else y + part
        gv = gv_next
    out = x + ga_ref[0] * y
    if final:
        out = out * lax.rsqrt(jnp.mean(out * out, axis=-1, keepdims=True) + EPS) * fg_ref[...]
    o_ref[0] = out


def _convglu(x, g, sc, sh, w_in, conv_w, conv_b, w_out, ga, final_g=None, tm=512):
    b, l, d = x.shape
    f = w_out.shape[0]
    tm = min(tm, l)
    sub = MXU_DIM if f % MXU_DIM == 0 else LANES
    hl = BF16_ROWS
    hb = tm // hl
    last_hb = l // hl - 1
    final = final_g is not None
    fg = (final_g if final else jnp.ones((d,), F32)).reshape(1, d)
    const = lambda shape: pl.BlockSpec(shape, lambda bi, i: (0,) * len(shape))
    mod = pl.BlockSpec((1, 1, d), lambda bi, i: (bi, 0, 0))
    return pl.pallas_call(
        functools.partial(_convglu_kernel, sub=sub, final=final),
        grid=(b, l // tm),
        in_specs=[pl.BlockSpec((1, tm, d), lambda bi, i: (bi, i, 0)),
                  pl.BlockSpec((1, hl, d), lambda bi, i: (bi, jnp.maximum(i * hb - 1, 0), 0)),
                  pl.BlockSpec((1, hl, d), lambda bi, i: (bi, jnp.minimum((i + 1) * hb, last_hb), 0)),
                  const((1, d)), mod, mod, const((d, 2 * f)), const((FFN_CONV_W, f)),
                  const((1, f)), const((f, d)), mod, const((1, d))],
        out_specs=pl.BlockSpec((1, tm, d), lambda bi, i: (bi, i, 0)),
        out_shape=jax.ShapeDtypeStruct((b, l, d), F32),
        compiler_params=_cparams("parallel", "parallel"),
        name="convglu",
    )(x, x, x, g.reshape(1, d), sc, sh, w_in, conv_w, conv_b.reshape(1, f), w_out, ga, fg)


def _gdn_in_kernel(x_ref, xp_ref, xn_ref, g_ref, sc_ref, sh_ref, w_ref, wz_ref, wab_ref, cw_ref,
                   qkv_ref, z_ref, ab_ref, *, n_norm, sub):
    i = pl.program_id(1)
    tm = x_ref.shape[1]
    hl = xp_ref.shape[1]
    ch = cw_ref.shape[1]
    pad = GDN_CONV_W // 2
    xe = jnp.concatenate([xp_ref[0], x_ref[0], xn_ref[0]], axis=0)
    he = _norm_mod(xe, g_ref[...], sc_ref[0], sh_ref[0]).astype(BF16)
    h = he[hl:hl + tm]
    r = lax.broadcasted_iota(jnp.int32, (tm + 2 * hl, 1), 0)
    inside = jnp.logical_and(jnp.logical_or(r >= hl, i > 0),
                             jnp.logical_or(r < hl + tm, i < pl.num_programs(1) - 1))
    keep = jnp.where(inside, 1.0, 0.0)

    def project(j):
        return jnp.dot(he, w_ref[:, j * sub:(j + 1) * sub], preferred_element_type=F32) * keep

    def act(pre, j):
        for hh in range(sub // LANES):
            lo = j * sub + hh * LANES
            cw = cw_ref[:, lo:lo + LANES]
            src = pre[:, hh * LANES:(hh + 1) * LANES]
            y = None
            for t in range(GDN_CONV_W):
                tap = src if t == pad else pltpu.roll(src, (pad - t) % src.shape[0], 0)
                term = cw[t:t + 1] * tap[hl:hl + tm]
                y = term if y is None else y + term
            y = _silu(y)
            if lo // LANES < n_norm:
                y = y * lax.rsqrt(jnp.sum(y * y, axis=-1, keepdims=True) + EPS)
            qkv_ref[0, :, lo:lo + LANES] = y.astype(qkv_ref.dtype)

    n = ch // sub
    pre = project(0)
    for j in range(n):
        pre_next = project(j + 1) if j + 1 < n else None
        if j == n - 1:
            z_ref[0] = jnp.dot(h, wz_ref[...], preferred_element_type=F32).astype(z_ref.dtype)
            ab_ref[0] = jnp.dot(h, wab_ref[...], preferred_element_type=F32)
        act(pre, j)
        pre = pre_next


def _gdn_in(x, g, sc, sh, w_qkv, w_z, w_ab, conv_w, n_norm, tm=512):
    b, l, d = x.shape
    ch, zw = w_qkv.shape[1], w_z.shape[1]
    tm = min(tm, l)
    hl = BF16_ROWS
    hb = tm // hl
    last_hb = l // hl - 1
    const = lambda shape: pl.BlockSpec(shape, lambda bi, i: (0,) * len(shape))
    mod = pl.BlockSpec((1, 1, d), lambda bi, i: (bi, 0, 0))
    rows = lambda w: pl.BlockSpec((1, tm, w), lambda bi, i: (bi, i, 0))
    return pl.pallas_call(
        functools.partial(_gdn_in_kernel, n_norm=n_norm, sub=MXU_DIM),
        grid=(b, l // tm),
        in_specs=[rows(d),
                  pl.BlockSpec((1, hl, d), lambda bi, i: (bi, jnp.maximum(i * hb - 1, 0), 0)),
                  pl.BlockSpec((1, hl, d), lambda bi, i: (bi, jnp.minimum((i + 1) * hb, last_hb), 0)),
                  const((1, d)), mod, mod, const((d, ch)), const((d, zw)), const((d, LANES)),
                  const((GDN_CONV_W, ch))],
        out_specs=[rows(ch), rows(zw), rows(LANES)],
        out_shape=[jax.ShapeDtypeStruct((b, l, ch), BF16), jax.ShapeDtypeStruct((b, l, zw), BF16),
                   jax.ShapeDtypeStruct((b, l, LANES), F32)],
        compiler_params=_cparams("parallel", "parallel"),
        name="gdn_in",
    )(x, x, x, g.reshape(1, d), sc, sh, w_qkv, w_z, w_ab, conv_w)


def _mm(xs, ys):
    return [jnp.dot(x, y, preferred_element_type=F32) for x, y in zip(xs, ys)]


def _bf(xs):
    return [x.astype(BF16) for x in xs]


def _neumann_minus_eye(ms, order):
    mb = _bf(ms)
    m2 = _mm(mb, mb)
    pb = _bf(m2)
    m3 = _mm(mb, pb)
    r = [b - a - c for a, b, c in zip(ms, m2, m3)]
    p, mp = 4, m2
    while p < order:
        mp = _mm(pb, pb)
        pb = _bf(mp)
        rm = _mm(_bf(r), pb)
        r = [x + y + z for x, y, z in zip(r, mp, rm)]
        p *= 2
    return r


def _unit_tri_inverse_minus_eye(a_list, same_block):
    c = a_list[0].shape[0]
    d = [jnp.where(same_block, a, 0.0) for a in a_list]
    lo = [a - x for a, x in zip(a_list, d)]
    r0 = _neumann_minus_eye(d, GDN_INV_BLOCK)
    r0b = _bf(r0)
    n = [x + y for x, y in zip(lo, _mm(r0b, _bf(lo)))]
    rn = _neumann_minus_eye(n, c // GDN_INV_BLOCK)
    return [x + y + z for x, y, z in zip(r0, rn, _mm(_bf(rn), r0b))]


def _gdn_core_kernel(qf_ref, kf_ref, vf_ref, abf_ref, qb_ref, kb_ref, vb_ref, abb_ref,
                     alog_ref, dtb_ref, s0_ref, of_ref, ob_ref, sout_ref, s_ref, gt_ref, *, hb):
    t = pl.program_id(2)
    grp = pl.program_id(1)
    rep = GDN_V_HEADS // GDN_QK_HEADS
    c, dk = GDN_CHUNK, GDN_HEAD_DIM
    scale = dk ** -0.5

    @pl.when(t == 0)
    def _():
        s_ref[...] = s0_ref[0]

    rr = lax.broadcasted_iota(jnp.int32, (c, c), 0)
    cc = lax.broadcasted_iota(jnp.int32, (c, c), 1)
    same_block = (rr // GDN_INV_BLOCK) == (cc // GDN_INV_BLOCK)
    lane = lax.broadcasted_iota(jnp.int32, (c, LANES), 1)
    ng = 2 * GDN_V_HEADS
    dirs = ((qf_ref, kf_ref, vf_ref, abf_ref, of_ref, rr >= cc, rr > cc),
            (qb_ref, kb_ref, vb_ref, abb_ref, ob_ref, rr <= cc, rr < cc))

    gcs, balls, kks, qks, kts = [], [], [], [], []
    for q_ref, k_ref, _, ab_ref, _, incl, _ in dirs:
        ab = ab_ref[0]
        sp = ab + dtb_ref[...]
        sp = jnp.maximum(sp, 0.0) + jnp.log1p(jnp.exp(-jnp.abs(sp)))
        gall = -jnp.exp(alog_ref[...]) * sp
        balls.append(_sigmoid(ab))
        g1 = gall.astype(BF16)
        r1 = gall - g1.astype(F32)
        g2 = r1.astype(BF16)
        g3 = (r1 - g2.astype(F32)).astype(BF16)
        packed = jnp.where(lane < ng, g1.astype(F32),
                           jnp.where(lane < 2 * ng, pltpu.roll(g2.astype(F32), ng, 1),
                                     jnp.where(lane < 3 * ng,
                                               pltpu.roll(g3.astype(F32), 2 * ng, 1), 0.0)))
        sel = jnp.concatenate([jnp.where(incl, 1.0, 0.0), jnp.ones((c, c), F32)], axis=0)
        cs = _dot(sel, packed)
        cs = cs + pltpu.roll(cs, LANES - ng, 1) + pltpu.roll(cs, LANES - 2 * ng, 1)
        gcs.append(cs[:c])
        d = len(gcs) - 1
        gt_ref[d, 0] = cs[:c].T
        gt_ref[d, 1] = cs[c:].T
        for hh in range(hb):
            sl = slice(hh * dk, (hh + 1) * dk)
            k = k_ref[0, :, sl]
            g = _dot_nt(jnp.concatenate([k, q_ref[0, :, sl]], axis=0), k)
            kks.append(g[:c])
            qks.append(g[c:])
            kts.append(k.astype(F32).T)

    chains = [(d, hh, e) for d in range(2) for hh in range(hb) for e in range(rep)]
    a_list, xs, qkms, qds, kdts, gls = [], [], [], [], [], []
    for d, hh, e in chains:
        q_ref, k_ref, v_ref, _, _, incl, strict = dirs[d]
        col = d * GDN_V_HEADS + (grp * hb + hh) * rep + e
        pick = lambda m, j: jnp.sum(jnp.where(lane == j, m, 0.0), axis=-1, keepdims=True)
        gc = pick(gcs[d], col)
        bcol = pick(balls[d], col + ng)
        gc_row = gt_ref[d, 0, pl.ds(col, 1), :]
        gtot_row = gt_ref[d, 1, pl.ds(col, 1), :]
        decay = jnp.where(incl, jnp.exp(jnp.minimum(gc - gc_row, 0.0)), 0.0)
        a_list.append(jnp.where(strict, bcol * kks[d * hb + hh] * decay, 0.0))
        kf = k_ref[0, :, hh * dk:(hh + 1) * dk].astype(F32)
        qf = q_ref[0, :, hh * dk:(hh + 1) * dk].astype(F32)
        vf = v_ref[0, :, (hh * rep + e) * dk:(hh * rep + e + 1) * dk].astype(F32)
        egc = jnp.exp(gc)
        xs.append(jnp.concatenate([vf * bcol, kf * (bcol * egc)], axis=1))
        qkms.append((qks[d * hb + hh] * decay * scale).astype(BF16))
        qds.append(qf * (egc * scale))
        kdts.append((kts[d * hb + hh] * jnp.exp(gtot_row - gc_row)).astype(BF16))
        gls.append(jnp.exp(gtot_row[:, :dk]))

    tms = _unit_tri_inverse_minus_eye(a_list, same_block)
    uws = [x + y for x, y in zip(xs, _mm(_bf(tms), _bf(xs)))]

    ss = [s_ref[i] for i in range(len(chains))]
    sbs = _bf(ss)
    wss = _mm([jnp.concatenate([uw[:, dk:], qd], axis=0).astype(BF16) for uw, qd in zip(uws, qds)],
              sbs)
    vns = [(uw[:, :dk] - ws[:c]).astype(BF16) for uw, ws in zip(uws, wss)]
    ovs = _mm(qkms, vns)
    kvs = _mm(kdts, vns)
    for i, (d, hh, e) in enumerate(chains):
        o_ref = dirs[d][4]
        o_ref[0, :, (hh * rep + e) * dk:(hh * rep + e + 1) * dk] = (
            wss[i][c:] + ovs[i]).astype(o_ref.dtype)
        s_ref[i] = ss[i] * gls[i] + kvs[i]

    @pl.when(t == pl.num_programs(2) - 1)
    def _():
        sout_ref[0] = s_ref[...]


def _gdn_core(qkv, ab, alog_row, dtb_row, s0, hb=GDN_HEADS_PER_STEP):
    b, l, _ = qkv.shape
    c = GDN_CHUNK
    nt = l // c
    hq, hv = GDN_QK_HEADS, GDN_V_HEADS
    rep = hv // hq
    ngrp = hq // hb
    fwd = lambda off: (lambda bi, g, t: (bi, t, off + g))
    bwd = lambda off: (lambda bi, g, t: (bi, nt - 1 - t, off + g))
    qw, vw = hb * LANES, hb * rep * LANES

    def specs(mk, ab_map):
        return [pl.BlockSpec((1, c, qw), mk(0)),
                pl.BlockSpec((1, c, qw), mk(ngrp)),
                pl.BlockSpec((1, c, vw), mk(ngrp)),
                pl.BlockSpec((1, c, LANES), ab_map)]

    ab_fwd = lambda bi, g, t: (bi, t, 0)
    ab_bwd = lambda bi, g, t: (bi, nt - 1 - t, 0)
    row = pl.BlockSpec((1, LANES), lambda bi, g, t: (0, 0))
    st = pl.BlockSpec((1, 2 * hb * rep, LANES, LANES), lambda bi, g, t: (bi * ngrp + g, 0, 0, 0))
    s0g = s0.reshape(b, 2, ngrp, hb * rep, LANES, LANES).transpose(0, 2, 1, 3, 4, 5)
    s0g = s0g.reshape(b * ngrp, 2 * hb * rep, LANES, LANES)
    of, ob, sg = pl.pallas_call(
        functools.partial(_gdn_core_kernel, hb=hb),
        grid=(b, ngrp, nt),
        in_specs=specs(fwd, ab_fwd) + specs(bwd, ab_bwd) + [row, row, st],
        out_specs=[pl.BlockSpec((1, c, vw), fwd(0)), pl.BlockSpec((1, c, vw), bwd(0)), st],
        out_shape=[jax.ShapeDtypeStruct((b, l, hv * LANES), BF16),
                   jax.ShapeDtypeStruct((b, l, hv * LANES), BF16),
                   jax.ShapeDtypeStruct(s0g.shape, F32)],
        scratch_shapes=[pltpu.VMEM((2 * hb * rep, LANES, LANES), F32),
                        pltpu.VMEM((2, 2, LANES, c), F32)],
        compiler_params=_cparams("parallel", "parallel", "arbitrary"),
        name="gdn_core",
    )(qkv, qkv, qkv, ab, qkv, qkv, qkv, ab, alog_row, dtb_row, s0g)
    sg = sg.reshape(b, ngrp, 2, hb * rep, LANES, LANES).transpose(0, 2, 1, 3, 4, 5)
    return of, ob, sg.reshape(b, 2, hv, LANES, LANES)


def _rotary(x, tabs):
    c_ref, sa_ref, sb_ref = tabs
    return (x * c_ref[...] + pltpu.roll(x, 96, 1) * sa_ref[...]
            + pltpu.roll(x, 32, 1) * sb_ref[...])


def _ret_core_kernel(*refs, hp, rope):
    qf_ref, kf_ref, vf_ref, qb_ref, kb_ref, vb_ref = refs[:6]
    tabs_f, tabs_b = (refs[6:9], refs[9:12]) if rope else (None, None)
    lg_ref, s0_ref, of_ref, ob_ref, sout_ref, s_ref, mask_ref, dec_ref = refs[12 if rope else 6:]
    t = pl.program_id(2)
    c = qf_ref.shape[1]
    dk, dv = RET_QK_DIM, RET_V_DIM
    scale = dk ** -0.5

    @pl.when(t == 0)
    def _():
        rr = lax.broadcasted_iota(jnp.int32, (c, c), 0)
        cc = lax.broadcasted_iota(jnp.int32, (c, c), 1)
        diff = (rr - cc).astype(F32)
        pos = lax.broadcasted_iota(jnp.int32, (c, LANES), 0).astype(F32)
        for h in range(hp):
            lgf, lgb = lg_ref[h, 0:1, :], lg_ref[h, 1:2, :]
            s_ref[2 * h] = s0_ref[0, h, 0]
            s_ref[2 * h + 1] = s0_ref[0, h, 1]
            mask_ref[h] = scale * jnp.where(diff > 0, jnp.exp(diff * lgf[:, :c]),
                                            jnp.where(diff < 0, jnp.exp(-diff * lgb[:, :c]), 2.0))
            lgf1, lgb1 = lgf[:, :LANES], lgb[:, :LANES]
            dec_ref[4 * h] = scale * jnp.exp((pos + 1.0) * lgf1)
            dec_ref[4 * h + 1] = jnp.exp((c - 1.0 - pos) * lgf1)
            dec_ref[4 * h + 2] = scale * jnp.exp((c - pos) * lgb1)
            dec_ref[4 * h + 3] = jnp.exp(pos * lgb1)

    def load(ref, h, tabs):
        x = ref[0, :, h * dk:(h + 1) * dk].astype(F32)
        return _rotary(x, tabs) if rope else x

    heads = []
    for h in range(hp):
        q, k = load(qf_ref, h, tabs_f), load(kf_ref, h, tabs_f)
        qb, kb = load(qb_ref, h, tabs_b), load(kb_ref, h, tabs_b)
        v, vb = vf_ref[0, :, h * dv:(h + 1) * dv], vb_ref[0, :, h * dv:(h + 1) * dv]
        sf, sb = s_ref[2 * h], s_ref[2 * h + 1]
        heads.append((v, sf, sb,
                      _dot_nt(q, k),
                      _dot(q * dec_ref[4 * h], sf),
                      _dot((k * dec_ref[4 * h + 1]).T, v),
                      _dot(qb * dec_ref[4 * h + 2], sb),
                      _dot((kb * dec_ref[4 * h + 3]).T, vb)))
    for h, (v, sf, sb, sc, inter_f, kv_f, inter_b, kv_b) in enumerate(heads):
        sl = slice(h * dv, (h + 1) * dv)
        of_ref[0, :, sl] = (_dot(sc * mask_ref[h], v) + inter_f).astype(of_ref.dtype)
        ob_ref[0, :, sl] = inter_b.astype(ob_ref.dtype)
        s_ref[2 * h] = sf * jnp.exp(c * lg_ref[h, 0:1, :]) + kv_f
        s_ref[2 * h + 1] = sb * jnp.exp(c * lg_ref[h, 1:2, :]) + kv_b

    @pl.when(t == pl.num_programs(2) - 1)
    def _():
        for h in range(hp):
            sout_ref[0, h, 0] = s_ref[2 * h]
            sout_ref[0, h, 1] = s_ref[2 * h + 1]


def _ret_core(pre, tables, lg, s0, hp=RET_HEADS_PER_STEP):
    b, l, _ = pre.shape
    c = min(RET_CHUNK, l)
    nt = l // c
    h_, dk, dv = RET_HEADS, RET_QK_DIM, RET_V_DIM
    ng = h_ // hp
    fwd = lambda off: (lambda bi, g, t: (bi, t, off + g))
    bwd = lambda off: (lambda bi, g, t: (bi, nt - 1 - t, off + g))

    def specs(mk):
        return [pl.BlockSpec((1, c, hp * dk), mk(0)), pl.BlockSpec((1, c, hp * dk), mk(ng)),
                pl.BlockSpec((1, c, hp * dv), mk(ng))]

    rope = tables is not None
    tab_specs, tab_args = [], []
    if rope:
        tab_specs = ([pl.BlockSpec((c, LANES), lambda bi, g, t: (t, 0))] * 3
                     + [pl.BlockSpec((c, LANES), lambda bi, g, t: (nt - 1 - t, 0))] * 3)
        tab_args = list(tables) * 2
    st = pl.BlockSpec((1, hp, 2, dk, dv), lambda bi, g, t: (bi, g, 0, 0, 0))
    return pl.pallas_call(
        functools.partial(_ret_core_kernel, hp=hp, rope=rope),
        grid=(b, ng, nt),
        in_specs=(specs(fwd) + specs(bwd) + tab_specs
                  + [pl.BlockSpec((hp, 2, dv), lambda bi, g, t: (g, 0, 0)), st]),
        out_specs=[pl.BlockSpec((1, c, hp * dv), fwd(0)), pl.BlockSpec((1, c, hp * dv), bwd(0)), st],
        out_shape=[jax.ShapeDtypeStruct((b, l, h_ * dv), BF16),
                   jax.ShapeDtypeStruct((b, l, h_ * dv), BF16),
                   jax.ShapeDtypeStruct((b, h_, 2, dk, dv), F32)],
        scratch_shapes=[pltpu.VMEM((2 * hp, dk, dv), F32), pltpu.VMEM((hp, c, c), F32),
                        pltpu.VMEM((4 * hp, c, LANES), F32)],
        compiler_params=_cparams("parallel", "parallel", "arbitrary"),
        name="retention_core",
    )(pre, pre, pre, pre, pre, pre, *tab_args, lg, s0)


def _attention_layer(x, xc, m, mc, norm_g, w_in, q_g, k_g, w_out, tables, want_ctx):
    hd = ATT_HEADS * ATT_HEAD_DIM
    kvd = ATT_KV_HEADS * ATT_HEAD_DIM
    w_in = w_in.astype(BF16)
    w_out = w_out.astype(BF16)
    gains = jnp.concatenate([jnp.tile(q_g * (ATT_HEAD_DIM ** -0.5 * LOG2E), (ATT_HEADS, 1)),
                             jnp.tile(k_g, (ATT_KV_HEADS, 1))], axis=0)

    def project(t, mm, tabs):
        qkv = _proj(t, norm_g, mm[1], mm[0], w_in, BF16, tm=1024, tn=hd + 2 * kvd)
        qk = _head_prep(qkv, gains, tabs, norm=True)
        return qk, jnp.concatenate([qk[:, :, hd:], qkv[:, :, hd + kvd:]], axis=-1)

    qk, kv = project(x, m, tables)
    qk_c, kv_c = project(xc, mc, None)
    o = _attention(qk, jnp.concatenate([kv, kv_c], axis=1))
    tm = min(512, x.shape[1])
    x = _out_res(_plain_prologue, [o], [_row_spec(tm, hd)], w_out, x, m[2], tm)
    if want_ctx:
        oc = _attention(qk_c, kv_c)
        tmc = min(512, xc.shape[1])
        xc = _out_res(_plain_prologue, [oc], [_row_spec(tmc, hd)], w_out, xc, mc[2], tmc)
    return x, xc


def _gdn_layer(x, xc, m, mc, norm_g, w_in, conv_w, a_log, dt_bias, gn_g, w_out, want_ctx):
    hq, hv, dh = GDN_QK_HEADS, GDN_V_HEADS, GDN_HEAD_DIM
    cw = (2 * hq + hv) * dh
    zw = hv * dh
    w_qkv = w_in[:, :cw].astype(BF16)
    w_z = w_in[:, cw:cw + zw].astype(BF16)
    w_ab = jnp.pad(w_in[:, cw + zw:], ((0, 0), (0, LANES - 4 * hv))).astype(BF16)
    w_out = w_out.astype(BF16)
    pad = (0, LANES - 2 * hv)
    alog_row = jnp.pad(a_log.reshape(-1), pad).reshape(1, LANES)
    dtb_row = jnp.pad(dt_bias.reshape(-1), pad).reshape(1, LANES)

    def project(t, mm):
        return _gdn_in(t, norm_g, mm[1], mm[0], w_qkv, w_z, w_ab, conv_w, n_norm=2 * hq)

    b = x.shape[0]
    qkv_c, z_c, ab_c = project(xc, mc)
    qkv, z, ab = project(x, m)
    s0 = jnp.zeros((b, 2, hv, dh, dh), F32)
    ocf, ocb, sc = _gdn_core(qkv_c, ab_c, alog_row, dtb_row, s0)
    of, ob, _ = _gdn_core(qkv, ab, alog_row, dtb_row, sc)

    def out(of_, ob_, z_, t, mm):
        tm = min(512, t.shape[1])
        gspec = pl.BlockSpec((1, LANES), lambda bi, i: (0, 0))
        return _out_res(_gdn_prologue, [of_, ob_, z_, gn_g.reshape(1, dh)],
                        [_row_spec(tm, zw)] * 3 + [gspec], w_out, t, mm[2], tm)

    x = out(of, ob, z, x, m)
    if want_ctx:
        xc = out(ocf, ocb, z_c, xc, mc)
    return x, xc


def _retention_layer(x, xc, m, mc, norm_g, w_in, decay, w_out, tables, want_ctx):
    h_, dk, dv = RET_HEADS, RET_QK_DIM, RET_V_DIM
    w_in = w_in.astype(BF16)
    w_out = w_out.astype(BF16)
    lg = jnp.broadcast_to(-decay.astype(F32).T[:, :, None], (h_, 2, dv))

    def project(t, mm):
        return _proj(t, norm_g, mm[1], mm[0], w_in, BF16, tm=1024, tn=2048)

    b = x.shape[0]
    vg_c = project(xc, mc)
    vg = project(x, m)
    s0 = jnp.zeros((b, h_, 2, dk, dv), F32)
    ocf, ocb, sc = _ret_core(vg_c, None, lg, s0)
    of, ob, _ = _ret_core(vg, tables, lg, sc)
    vw = h_ * dv

    def out(of_, ob_, vg_, t, mm):
        tm = min(512, t.shape[1])
        return _out_res(_ret_prologue, [of_, ob_, vg_],
                        [_row_spec(tm, vw), _row_spec(tm, vw), _row_spec(tm, vw, col=2)],
                        w_out, t, mm[2], tm)

    x = out(of, ob, vg, x, m)
    if want_ctx:
        xc = out(ocf, ocb, vg_c, xc, mc)
    return x, xc


def _ffn(x, m, norm_g, w_in, conv_w, conv_b, w_out, final_g=None):
    return _convglu(x, norm_g, m[4], m[3], w_in.astype(BF16), conv_w, conv_b, w_out.astype(BF16),
                    m[5], final_g)


def kernel(x, c, ctx, c_ctx, mod_w, mod_b, norm1_g, norm2_g, att_w_in, att_q_g, att_k_g, att_w_out, gdn_w_in, gdn_conv_w, gdn_A_log, gdn_dt_bias, gdn_norm_g, gdn_w_out, ret_w_in, ret_decay, ret_w_out, ffn_w_in, ffn_conv_w, ffn_conv_b, ffn_w_out, final_g):
    b, n, d = x.shape
    depth = mod_w.shape[0]
    cc = jnp.concatenate([c, c_ctx[None, :], jnp.zeros((8 - b - 1, d), F32)], axis=0)
    mods = _modulation(cc, mod_w, mod_b)
    tab_att = _rope_tables(n, ATT_HEAD_DIM)
    tab_ret = _rope_tables(n, RET_QK_DIM)
    xc = ctx
    for i in range(depth):
        last = i == depth - 1
        mi = mods[i].reshape(8, 6, d)
        m = [mi[:b, s][:, None, :] for s in range(6)]
        mc = [jnp.broadcast_to(mi[b, s][None, None, :], (b, 1, d)) for s in range(6)]
        kind, j = i % N_MIXERS, i // N_MIXERS
        if kind == 0:
            x, xc = _attention_layer(x, xc, m, mc, norm1_g[i], att_w_in[j], att_q_g[j], att_k_g[j],
                                     att_w_out[j], tab_att, not last)
        elif kind == 1:
            x, xc = _gdn_layer(x, xc, m, mc, norm1_g[i], gdn_w_in[j], gdn_conv_w[j], gdn_A_log[j],
                               gdn_dt_bias[j], gdn_norm_g[j], gdn_w_out[j], not last)
        else:
            x, xc = _retention_layer(x, xc, m, mc, norm1_g[i], ret_w_in[j], ret_decay[j],
                                     ret_w_out[j], tab_ret, not last)
        x = _ffn(x, m, norm2_g[i], ffn_w_in[i], ffn_conv_w[i], ffn_conv_b[i], ffn_w_out[i],
                 final_g if last else None)
        if not last:
            xc = _ffn(xc, mc, norm2_g[i], ffn_w_in[i], ffn_conv_w[i], ffn_conv_b[i], ffn_w_out[i])
    return x
```

```python
import functools

import jax
import jax.numpy as jnp
from jax import lax
from jax.experimental import pallas as pl
from jax.experimental.pallas import tpu as pltpu

F32 = jnp.float32
BF16 = jnp.bfloat16

EPS = 1e-6
LOG2E = 1.4426950408889634
ROPE_THETA = 10000.0
GRID_W = 64
N_MIXERS = 3

ATT_HEADS, ATT_KV_HEADS, ATT_HEAD_DIM = 8, 2, 128
GDN_QK_HEADS, GDN_V_HEADS, GDN_HEAD_DIM, GDN_CONV_W = 8, 16, 128, 5
RET_HEADS, RET_QK_DIM, RET_V_DIM = 8, 128, 256
FFN_CONV_W = 3

LANES = 128
BF16_ROWS = 16
MXU_DIM = 256
GDN_CHUNK = 128
GDN_INV_BLOCK = 16
GDN_HEADS_PER_STEP = 4
RET_CHUNK = 256
RET_HEADS_PER_STEP = 2
ATT_MAX_UNROLL = 11
VMEM_LIMIT = 56 * 1024 * 1024


def _cparams(*sem):
    return pltpu.CompilerParams(dimension_semantics=sem, vmem_limit_bytes=VMEM_LIMIT)


def _sigmoid(x):
    return 1.0 / (1.0 + jnp.exp(-x))


def _silu(x):
    return x * _sigmoid(x)


def _dot(a, b):
    return jnp.dot(a.astype(BF16), b.astype(BF16), preferred_element_type=F32)


def _dot_nt(a, b):
    return lax.dot_general(a.astype(BF16), b.astype(BF16), (((1,), (1,)), ((), ())),
                           preferred_element_type=F32)


def _mod_kernel(cc_ref, w_ref, b_ref, o_ref):
    s = _silu(cc_ref[...])
    o_ref[0] = _dot(s, w_ref[0]) + b_ref[0]


def _modulation(cc, mod_w, mod_b):
    depth, d, n6 = mod_w.shape
    tn = n6 // 4
    return pl.pallas_call(
        _mod_kernel,
        grid=(depth, n6 // tn),
        in_specs=[pl.BlockSpec((8, d), lambda l, j: (0, 0)),
                  pl.BlockSpec((1, d, tn), lambda l, j: (l, 0, j)),
                  pl.BlockSpec((1, 1, tn), lambda l, j: (l, 0, j))],
        out_specs=pl.BlockSpec((1, 8, tn), lambda l, j: (l, 0, j)),
        out_shape=jax.ShapeDtypeStruct((depth, 8, n6), F32),
        compiler_params=_cparams("parallel", "parallel"),
        name="modulation",
    )(cc, mod_w, mod_b.reshape(depth, 1, n6))


def _norm_mod(x, g, sc, sh):
    ms = jnp.mean(x * x, axis=-1, keepdims=True)
    return (x * lax.rsqrt(ms + EPS)) * g * (1.0 + sc) + sh


def _proj_kernel(x_ref, g_ref, sc_ref, sh_ref, w_ref, o_ref, h_ref):
    @pl.when(pl.program_id(2) == 0)
    def _():
        h_ref[...] = _norm_mod(x_ref[0], g_ref[...], sc_ref[0], sh_ref[0]).astype(BF16)

    o_ref[0] = jnp.dot(h_ref[...], w_ref[...], preferred_element_type=F32).astype(o_ref.dtype)


def _proj(x, g, sc, sh, w, out_dtype, tm, tn):
    b, l, d = x.shape
    n = w.shape[1]
    tm = min(tm, l)
    tn = min(tn, n)
    return pl.pallas_call(
        _proj_kernel,
        grid=(b, l // tm, n // tn),
        in_specs=[pl.BlockSpec((1, tm, d), lambda bi, i, j: (bi, i, 0)),
                  pl.BlockSpec((1, d), lambda bi, i, j: (0, 0)),
                  pl.BlockSpec((1, 1, d), lambda bi, i, j: (bi, 0, 0)),
                  pl.BlockSpec((1, 1, d), lambda bi, i, j: (bi, 0, 0)),
                  pl.BlockSpec((d, tn), lambda bi, i, j: (0, j))],
        out_specs=pl.BlockSpec((1, tm, tn), lambda bi, i, j: (bi, i, j)),
        out_shape=jax.ShapeDtypeStruct((b, l, n), out_dtype),
        scratch_shapes=[pltpu.VMEM((tm, d), BF16)],
        compiler_params=_cparams("parallel", "parallel", "arbitrary"),
        name="norm_mod_proj",
    )(x, g.reshape(1, d), sc, sh, w)


def _rope_tables(n, head_dim):
    rows = n // GRID_W
    row = jnp.repeat(jnp.arange(rows, dtype=F32), GRID_W)
    col = jnp.tile(jnp.arange(GRID_W, dtype=F32), rows)
    axis_dim = head_dim // 2
    inv_freq = ROPE_THETA ** (-jnp.arange(0, axis_dim, 2, dtype=F32) / axis_dim)
    ar, ac = row[:, None] * inv_freq, col[:, None] * inv_freq
    cr, sr, cc, sc = jnp.cos(ar), jnp.sin(ar), jnp.cos(ac), jnp.sin(ac)
    z = jnp.zeros_like(sr)
    return (jnp.concatenate([cr, cr, cc, cc], axis=-1),
            jnp.concatenate([-sr, z, -sc, z], axis=-1),
            jnp.concatenate([z, sr, z, sc], axis=-1))


def _head_prep_kernel(*refs, norm, rope):
    if rope:
        x_ref, g_ref, c_ref, sa_ref, sb_ref, o_ref = refs
    else:
        x_ref, g_ref, o_ref = refs
    for h in range(g_ref.shape[1]):
        sl = slice(h * LANES, (h + 1) * LANES)
        x = x_ref[0, :, sl].astype(F32)
        if norm:
            x = x * lax.rsqrt(jnp.mean(x * x, axis=-1, keepdims=True) + EPS)
        x = x * g_ref[0, h:h + 1, :]
        if rope:
            x = (x * c_ref[...] + pltpu.roll(x, 96, 1) * sa_ref[...]
                 + pltpu.roll(x, 32, 1) * sb_ref[...])
        o_ref[0, :, sl] = x.astype(o_ref.dtype)


def _head_prep(x, gains, tables, norm, tm=512):
    b, l, _ = x.shape
    nb = gains.shape[0]
    tm = min(tm, l)
    hw = max(h for h in (8, 5, 4, 2, 1) if nb % h == 0)
    in_specs = [pl.BlockSpec((1, tm, hw * LANES), lambda bi, i, c: (bi, i, c)),
                pl.BlockSpec((1, hw, LANES), lambda bi, i, c: (c, 0, 0))]
    args = [x, gains.reshape(nb // hw, hw, LANES)]
    if tables is not None:
        in_specs += [pl.BlockSpec((tm, LANES), lambda bi, i, c: (i, 0))] * 3
        args += list(tables)
    return pl.pallas_call(
        functools.partial(_head_prep_kernel, norm=norm, rope=tables is not None),
        grid=(b, l // tm, nb // hw),
        in_specs=in_specs,
        out_specs=pl.BlockSpec((1, tm, hw * LANES), lambda bi, i, c: (bi, i, c)),
        out_shape=jax.ShapeDtypeStruct((b, l, nb * LANES), BF16),
        compiler_params=_cparams("parallel", "parallel", "parallel"),
        name="head_prep",
    )(*args)


def _attn_kernel(q_ref, k_ref, vt_ref, o_ref, *, groups, tq, unroll):
    dh = LANES
    qt = jnp.concatenate([q_ref[0, :, g * dh:(g + 1) * dh].astype(F32).T for g in range(groups)],
                         axis=1).astype(BF16)
    rows = groups * tq
    nk, vrows, tk = vt_ref.shape[2:]

    def scores(j):
        off = pl.multiple_of(j * tk, tk)
        return jnp.dot(k_ref[0, pl.ds(off, tk), :], qt, preferred_element_type=F32)

    def body(it, carry):
        m, acc = carry
        base = it * unroll
        st = scores(base)
        pv = alpha = None
        for u in range(unroll):
            st_next = scores(base + u + 1) if u + 1 < unroll else None
            m_new = jnp.maximum(m, jnp.max(st, axis=0, keepdims=True))
            p = jnp.exp2(st - m_new).astype(BF16)
            if pv is not None:
                acc = alpha * acc + pv
            alpha = jnp.exp2(m - m_new)
            pv = jnp.dot(vt_ref[0, 0, base + u], p, preferred_element_type=F32)
            m, st = m_new, st_next
        return m, alpha * acc + pv

    init = (jnp.full((1, rows), -1e30, F32), jnp.zeros((vrows, rows), F32))
    _, acc = lax.fori_loop(0, nk // unroll, body, init)
    out = acc[:dh] / acc[dh:dh + 1]
    for g in range(groups):
        o_ref[0, :, g * dh:(g + 1) * dh] = out[:, g * tq:(g + 1) * tq].T.astype(o_ref.dtype)


def _attention(q, kv, tq=1024, tk=256):
    b, n, _ = q.shape
    lk = kv.shape[1]
    kvh = ATT_KV_HEADS
    groups = ATT_HEADS // kvh
    hd = ATT_HEADS * LANES
    gw = groups * LANES
    tq = min(tq, n)
    tk = min(tk, lk)
    nk = lk // tk
    vt = kv[:, :, kvh * LANES:].reshape(b, nk, tk, kvh, LANES).transpose(0, 3, 1, 4, 2)
    vt = jnp.concatenate([vt, jnp.ones((b, kvh, nk, BF16_ROWS, tk), BF16)], axis=3)
    vrows = LANES + BF16_ROWS
    unroll = max(u for u in range(1, ATT_MAX_UNROLL + 1) if nk % u == 0)
    return pl.pallas_call(
        functools.partial(_attn_kernel, groups=groups, tq=tq, unroll=unroll),
        grid=(b, kvh, n // tq),
        in_specs=[pl.BlockSpec((1, tq, gw), lambda bi, h, i: (bi, i, h)),
                  pl.BlockSpec((1, lk, LANES), lambda bi, h, i: (bi, 0, h)),
                  pl.BlockSpec((1, 1, nk, vrows, tk), lambda bi, h, i: (bi, h, 0, 0, 0))],
        out_specs=pl.BlockSpec((1, tq, gw), lambda bi, h, i: (bi, i, h)),
        out_shape=jax.ShapeDtypeStruct((b, n, hd), BF16),
        compiler_params=_cparams("parallel", "parallel", "parallel"),
        name="gqa_attention",
    )(q, kv, vt)


def _plain_prologue(j, a_ref):
    return a_ref[0, :, j * MXU_DIM:(j + 1) * MXU_DIM]


def _gdn_prologue(j, of_ref, ob_ref, z_ref, g_ref):
    outs = []
    g = g_ref[...]
    for h in range(j * MXU_DIM // LANES, (j + 1) * MXU_DIM // LANES):
        sl = slice(h * LANES, (h + 1) * LANES)
        o = of_ref[0, :, sl].astype(F32) + ob_ref[0, :, sl].astype(F32)
        o = o * lax.rsqrt(jnp.mean(o * o, axis=-1, keepdims=True) + EPS) * g
        outs.append((o * _silu(z_ref[0, :, sl].astype(F32))).astype(BF16))
    return jnp.concatenate(outs, axis=-1)


def _ret_prologue(j, of_ref, ob_ref, gate_ref):
    sl = slice(j * RET_V_DIM, (j + 1) * RET_V_DIM)
    o = of_ref[0, :, sl].astype(F32) + ob_ref[0, :, sl].astype(F32)
    o = o * lax.rsqrt(jnp.mean(o * o, axis=-1, keepdims=True) + EPS)
    return (o * _silu(gate_ref[0, :, sl].astype(F32))).astype(BF16)


def _out_res_kernel(*refs, prologue, n_in):
    w_ref, x_ref, ga_ref, o_ref = refs[n_in:]
    ins = refs[:n_in]
    n = w_ref.shape[0] // MXU_DIM
    a, y = prologue(0, *ins), None
    for j in range(n):
        a_next = prologue(j + 1, *ins) if j + 1 < n else None
        part = jnp.dot(a, w_ref[j * MXU_DIM:(j + 1) * MXU_DIM, :], preferred_element_type=F32)
        y = part if y is None else y + part
        a = a_next
    o_ref[0] = x_ref[0] + ga_ref[0] * y


def _out_res(prologue, ins, in_specs, w, x, ga, tm):
    b, l, d = x.shape
    k = w.shape[0]
    return pl.pallas_call(
        functools.partial(_out_res_kernel, prologue=prologue, n_in=len(ins)),
        grid=(b, l // tm),
        in_specs=in_specs + [pl.BlockSpec((k, d), lambda bi, i: (0, 0)),
                             pl.BlockSpec((1, tm, d), lambda bi, i: (bi, i, 0)),
                             pl.BlockSpec((1, 1, d), lambda bi, i: (bi, 0, 0))],
        out_specs=pl.BlockSpec((1, tm, d), lambda bi, i: (bi, i, 0)),
        out_shape=jax.ShapeDtypeStruct((b, l, d), F32),
        compiler_params=_cparams("parallel", "parallel"),
        name="out_proj_residual",
    )(*ins, w, x, ga)


def _row_spec(tm, width, col=0):
    return pl.BlockSpec((1, tm, width), lambda bi, i: (bi, i, col))


def _convglu_kernel(x_ref, xp_ref, xn_ref, g_ref, sc_ref, sh_ref, win_ref, cw_ref, cb_ref, wout_ref,
                    ga_ref, fg_ref, o_ref, *, sub, final):
    i = pl.program_id(1)
    tm = x_ref.shape[1]
    f = wout_ref.shape[0]
    hl = xp_ref.shape[1]
    x = x_ref[0]
    xe = jnp.concatenate([xp_ref[0], x, xn_ref[0]], axis=0)
    he = _norm_mod(xe, g_ref[...], sc_ref[0], sh_ref[0]).astype(BF16)
    h = he[hl:hl + tm]
    r = lax.broadcasted_iota(jnp.int32, (tm + 2 * hl, 1), 0)
    inside = jnp.logical_and(jnp.logical_or(r >= hl, i > 0),
                             jnp.logical_or(r < hl + tm, i < pl.num_programs(1) - 1))
    keep = jnp.where(inside, 1.0, 0.0)

    def project(j):
        gate = jnp.dot(he, win_ref[:, f + j * sub:f + (j + 1) * sub],
                       preferred_element_type=F32) * keep
        val = jnp.dot(h, win_ref[:, j * sub:(j + 1) * sub], preferred_element_type=F32)
        return gate, val

    def act(gate, val, j):
        sl = slice(j * sub, (j + 1) * sub)
        cw = cw_ref[:, sl]
        conv = (cw[0:1] * gate[hl - 1:hl - 1 + tm] + cw[1:2] * gate[hl:hl + tm]
                + cw[2:3] * gate[hl + 1:hl + 1 + tm] + cb_ref[:, sl])
        gelu = 0.5 * conv * (1.0 + lax.erf(conv * (2.0 ** -0.5)))
        return (gelu * val).astype(BF16)

    n = f // sub
    gv, y = project(0), None
    for j in range(n):
        gv_next = project(j + 1) if j + 1 < n else None
        a = act(*gv, j)
        part = jnp.dot(a, wout_ref[j * sub:(j + 1) * sub, :], preferred_element_type=F32)
        y = part if y is None else y + part
        gv = gv_next
    out = x + ga_ref[0] * y
    if final:
        out = out * lax.rsqrt(jnp.mean(out * out, axis=-1, keepdims=True) + EPS) * fg_ref[...]
    o_ref[0] = out


def _convglu(x, g, sc, sh, w_in, conv_w, conv_b, w_out, ga, final_g=None, tm=512):
    b, l, d = x.shape
    f = w_out.shape[0]
    tm = min(tm, l)
    sub = MXU_DIM if f % MXU_DIM == 0 else LANES
    hl = BF16_ROWS
    hb = tm // hl
    last_hb = l // hl - 1
    final = final_g is not None
    fg = (final_g if final else jnp.ones((d,), F32)).reshape(1, d)
    const = lambda shape: pl.BlockSpec(shape, lambda bi, i: (0,) * len(shape))
    mod = pl.BlockSpec((1, 1, d), lambda bi, i: (bi, 0, 0))
    return pl.pallas_call(
        functools.partial(_convglu_kernel, sub=sub, final=final),
        grid=(b, l // tm),
        in_specs=[pl.BlockSpec((1, tm, d), lambda bi, i: (bi, i, 0)),
                  pl.BlockSpec((1, hl, d), lambda bi, i: (bi, jnp.maximum(i * hb - 1, 0), 0)),
                  pl.BlockSpec((1, hl, d), lambda bi, i: (bi, jnp.minimum((i + 1) * hb, last_hb), 0)),
                  const((1, d)), mod, mod, const((d, 2 * f)), const((FFN_CONV_W, f)),
                  const((1, f)), const((f, d)), mod, const((1, d))],
        out_specs=pl.BlockSpec((1, tm, d), lambda bi, i: (bi, i, 0)),
        out_shape=jax.ShapeDtypeStruct((b, l, d), F32),
        compiler_params=_cparams("parallel", "parallel"),
        name="convglu",
    )(x, x, x, g.reshape(1, d), sc, sh, w_in, conv_w, conv_b.reshape(1, f), w_out, ga, fg)


def _gdn_in_kernel(x_ref, xp_ref, xn_ref, g_ref, sc_ref, sh_ref, w_ref, wz_ref, wab_ref, cw_ref,
                   qkv_ref, z_ref, ab_ref, *, n_norm, sub):
    i = pl.program_id(1)
    tm = x_ref.shape[1]
    hl = xp_ref.shape[1]
    ch = cw_ref.shape[1]
    pad = GDN_CONV_W // 2
    xe = jnp.concatenate([xp_ref[0], x_ref[0], xn_ref[0]], axis=0)
    he = _norm_mod(xe, g_ref[...], sc_ref[0], sh_ref[0]).astype(BF16)
    h = he[hl:hl + tm]
    r = lax.broadcasted_iota(jnp.int32, (tm + 2 * hl, 1), 0)
    inside = jnp.logical_and(jnp.logical_or(r >= hl, i > 0),
                             jnp.logical_or(r < hl + tm, i < pl.num_programs(1) - 1))
    keep = jnp.where(inside, 1.0, 0.0)

    def project(j):
        return jnp.dot(he, w_ref[:, j * sub:(j + 1) * sub], preferred_element_type=F32) * keep

    def act(pre, j):
        for hh in range(sub // LANES):
            lo = j * sub + hh * LANES
            cw = cw_ref[:, lo:lo + LANES]
            src = pre[:, hh * LANES:(hh + 1) * LANES]
            y = None
            for t in range(GDN_CONV_W):
                tap = src if t == pad else pltpu.roll(src, (pad - t) % src.shape[0], 0)
                term = cw[t:t + 1] * tap[hl:hl + tm]
                y = term if y is None else y + term
            y = _silu(y)
            if lo // LANES < n_norm:
                y = y * lax.rsqrt(jnp.sum(y * y, axis=-1, keepdims=True) + EPS)
            qkv_ref[0, :, lo:lo + LANES] = y.astype(qkv_ref.dtype)

    n = ch // sub
    pre = project(0)
    for j in range(n):
        pre_next = project(j + 1) if j + 1 < n else None
        if j == n - 1:
            z_ref[0] = jnp.dot(h, wz_ref[...], preferred_element_type=F32).astype(z_ref.dtype)
            ab_ref[0] = jnp.dot(h, wab_ref[...], preferred_element_type=F32)
        act(pre, j)
        pre = pre_next


def _gdn_in(x, g, sc, sh, w_qkv, w_z, w_ab, conv_w, n_norm, tm=512):
    b, l, d = x.shape
    ch, zw = w_qkv.shape[1], w_z.shape[1]
    tm = min(tm, l)
    hl = BF16_ROWS
    hb = tm // hl
    last_hb = l // hl - 1
    const = lambda shape: pl.BlockSpec(shape, lambda bi, i: (0,) * len(shape))
    mod = pl.BlockSpec((1, 1, d), lambda bi, i: (bi, 0, 0))
    rows = lambda w: pl.BlockSpec((1, tm, w), lambda bi, i: (bi, i, 0))
    return pl.pallas_call(
        functools.partial(_gdn_in_kernel, n_norm=n_norm, sub=MXU_DIM),
        grid=(b, l // tm),
        in_specs=[rows(d),
                  pl.BlockSpec((1, hl, d), lambda bi, i: (bi, jnp.maximum(i * hb - 1, 0), 0)),
                  pl.BlockSpec((1, hl, d), lambda bi, i: (bi, jnp.minimum((i + 1) * hb, last_hb), 0)),
                  const((1, d)), mod, mod, const((d, ch)), const((d, zw)), const((d, LANES)),
                  const((GDN_CONV_W, ch))],
        out_specs=[rows(ch), rows(zw), rows(LANES)],
        out_shape=[jax.ShapeDtypeStruct((b, l, ch), BF16), jax.ShapeDtypeStruct((b, l, zw), BF16),
                   jax.ShapeDtypeStruct((b, l, LANES), F32)],
        compiler_params=_cparams("parallel", "parallel"),
        name="gdn_in",
    )(x, x, x, g.reshape(1, d), sc, sh, w_qkv, w_z, w_ab, conv_w)


def _mm(xs, ys):
    return [jnp.dot(x, y, preferred_element_type=F32) for x, y in zip(xs, ys)]


def _bf(xs):
    return [x.astype(BF16) for x in xs]


def _neumann_minus_eye(ms, order):
    mb = _bf(ms)
    m2 = _mm(mb, mb)
    pb = _bf(m2)
    m3 = _mm(mb, pb)
    r = [b - a - c for a, b, c in zip(ms, m2, m3)]
    p, mp = 4, m2
    while p < order:
        mp = _mm(pb, pb)
        pb = _bf(mp)
        rm = _mm(_bf(r), pb)
        r = [x + y + z for x, y, z in zip(r, mp, rm)]
        p *= 2
    return r


def _unit_tri_inverse_minus_eye(a_list, same_block):
    c = a_list[0].shape[0]
    d = [jnp.where(same_block, a, 0.0) for a in a_list]
    lo = [a - x for a, x in zip(a_list, d)]
    r0 = _neumann_minus_eye(d, GDN_INV_BLOCK)
    r0b = _bf(r0)
    n = [x + y for x, y in zip(lo, _mm(r0b, _bf(lo)))]
    rn = _neumann_minus_eye(n, c // GDN_INV_BLOCK)
    return [x + y + z for x, y, z in zip(r0, rn, _mm(_bf(rn), r0b))]


def _gdn_core_kernel(qf_ref, kf_ref, vf_ref, abf_ref, qb_ref, kb_ref, vb_ref, abb_ref,
                     alog_ref, dtb_ref, s0_ref, of_ref, ob_ref, sout_ref, s_ref, gt_ref, *, hb):
    t = pl.program_id(2)
    grp = pl.program_id(1)
    rep = GDN_V_HEADS // GDN_QK_HEADS
    c, dk = GDN_CHUNK, GDN_HEAD_DIM
    scale = dk ** -0.5

    @pl.when(t == 0)
    def _():
        s_ref[...] = s0_ref[0]

    rr = lax.broadcasted_iota(jnp.int32, (c, c), 0)
    cc = lax.broadcasted_iota(jnp.int32, (c, c), 1)
    same_block = (rr // GDN_INV_BLOCK) == (cc // GDN_INV_BLOCK)
    lane = lax.broadcasted_iota(jnp.int32, (c, LANES), 1)
    ng = 2 * GDN_V_HEADS
    dirs = ((qf_ref, kf_ref, vf_ref, abf_ref, of_ref, rr >= cc, rr > cc),
            (qb_ref, kb_ref, vb_ref, abb_ref, ob_ref, rr <= cc, rr < cc))

    gcs, balls, kks, qks, kts = [], [], [], [], []
    for q_ref, k_ref, _, ab_ref, _, incl, _ in dirs:
        ab = ab_ref[0]
        sp = ab + dtb_ref[...]
        sp = jnp.maximum(sp, 0.0) + jnp.log1p(jnp.exp(-jnp.abs(sp)))
        gall = -jnp.exp(alog_ref[...]) * sp
        balls.append(_sigmoid(ab))
        g1 = gall.astype(BF16)
        r1 = gall - g1.astype(F32)
        g2 = r1.astype(BF16)
        g3 = (r1 - g2.astype(F32)).astype(BF16)
        packed = jnp.where(lane < ng, g1.astype(F32),
                           jnp.where(lane < 2 * ng, pltpu.roll(g2.astype(F32), ng, 1),
                                     jnp.where(lane < 3 * ng,
                                               pltpu.roll(g3.astype(F32), 2 * ng, 1), 0.0)))
        sel = jnp.concatenate([jnp.where(incl, 1.0, 0.0), jnp.ones((c, c), F32)], axis=0)
        cs = _dot(sel, packed)
        cs = cs + pltpu.roll(cs, LANES - ng, 1) + pltpu.roll(cs, LANES - 2 * ng, 1)
        gcs.append(cs[:c])
        d = len(gcs) - 1
        gt_ref[d, 0] = cs[:c].T
        gt_ref[d, 1] = cs[c:].T
        for hh in range(hb):
            sl = slice(hh * dk, (hh + 1) * dk)
            k = k_ref[0, :, sl]
            g = _dot_nt(jnp.concatenate([k, q_ref[0, :, sl]], axis=0), k)
            kks.append(g[:c])
            qks.append(g[c:])
            kts.append(k.astype(F32).T)

    chains = [(d, hh, e) for d in range(2) for hh in range(hb) for e in range(rep)]
    a_list, xs, qkms, qds, kdts, gls = [], [], [], [], [], []

    def prepare(d, hh, e):
        q_ref, k_ref, v_ref, _, _, incl, strict = dirs[d]
        col = d * GDN_V_HEADS + (grp * hb + hh) * rep + e
        pick = lambda m, j: jnp.sum(jnp.where(lane == j, m, 0.0), axis=-1, keepdims=True)
        gc = pick(gcs[d], col)
        bcol = pick(balls[d], col + ng)
        gc_row = gt_ref[d, 0, pl.ds(col, 1), :]
        gtot_row = gt_ref[d, 1, pl.ds(col, 1), :]
        decay = jnp.where(incl, jnp.exp(jnp.minimum(gc - gc_row, 0.0)), 0.0)
        a_list.append(jnp.where(strict, bcol * kks[d * hb + hh] * decay, 0.0))
        kf = k_ref[0, :, hh * dk:(hh + 1) * dk].astype(F32)
        qf = q_ref[0, :, hh * dk:(hh + 1) * dk].astype(F32)
        vf = v_ref[0, :, (hh * rep + e) * dk:(hh * rep + e + 1) * dk].astype(F32)
        egc = jnp.exp(gc)
        xs.append(jnp.concatenate([vf * bcol, kf * (bcol * egc)], axis=1))
        qkms.append((qks[d * hb + hh] * decay * scale).astype(BF16))
        qds.append(qf * (egc * scale))
        kdts.append((kts[d * hb + hh] * jnp.exp(gtot_row - gc_row)).astype(BF16))
        gls.append(jnp.exp(gtot_row[:, :dk]))

    for ch in chains:
        prepare(*ch)

    tms = _unit_tri_inverse_minus_eye(a_list, same_block)
    uws = [x + y for x, y in zip(xs, _mm(_bf(tms), _bf(xs)))]

    ss = [s_ref[i] for i in range(len(chains))]
    sbs = _bf(ss)
    wss = _mm([jnp.concatenate([uw[:, dk:], qd], axis=0).astype(BF16) for uw, qd in zip(uws, qds)],
              sbs)
    vns = [(uw[:, :dk] - ws[:c]).astype(BF16) for uw, ws in zip(uws, wss)]
    ovs = _mm(qkms, vns)
    kvs = _mm(kdts, vns)
    for i, (d, hh, e) in enumerate(chains):
        o_ref = dirs[d][4]
        o_ref[0, :, (hh * rep + e) * dk:(hh * rep + e + 1) * dk] = (
            wss[i][c:] + ovs[i]).astype(o_ref.dtype)
        s_ref[i] = ss[i] * gls[i] + kvs[i]

    @pl.when(t == pl.num_programs(2) - 1)
    def _():
        sout_ref[0] = s_ref[...]


def _gdn_core(qkv, ab, alog_row, dtb_row, s0, hb=GDN_HEADS_PER_STEP):
    b, l, _ = qkv.shape
    c = GDN_CHUNK
    nt = l // c
    hq, hv = GDN_QK_HEADS, GDN_V_HEADS
    rep = hv // hq
    ngrp = hq // hb
    fwd = lambda off: (lambda bi, g, t: (bi, t, off + g))
    bwd = lambda off: (lambda bi, g, t: (bi, nt - 1 - t, off + g))
    qw, vw = hb * LANES, hb * rep * LANES

    def specs(mk, ab_map):
        return [pl.BlockSpec((1, c, qw), mk(0)),
                pl.BlockSpec((1, c, qw), mk(ngrp)),
                pl.BlockSpec((1, c, vw), mk(ngrp)),
                pl.BlockSpec((1, c, LANES), ab_map)]

    ab_fwd = lambda bi, g, t: (bi, t, 0)
    ab_bwd = lambda bi, g, t: (bi, nt - 1 - t, 0)
    row = pl.BlockSpec((1, LANES), lambda bi, g, t: (0, 0))
    st = pl.BlockSpec((1, 2 * hb * rep, LANES, LANES), lambda bi, g, t: (bi * ngrp + g, 0, 0, 0))
    s0g = s0.reshape(b, 2, ngrp, hb * rep, LANES, LANES).transpose(0, 2, 1, 3, 4, 5)
    s0g = s0g.reshape(b * ngrp, 2 * hb * rep, LANES, LANES)
    of, ob, sg = pl.pallas_call(
        functools.partial(_gdn_core_kernel, hb=hb),
        grid=(b, ngrp, nt),
        in_specs=specs(fwd, ab_fwd) + specs(bwd, ab_bwd) + [row, row, st],
        out_specs=[pl.BlockSpec((1, c, vw), fwd(0)), pl.BlockSpec((1, c, vw), bwd(0)), st],
        out_shape=[jax.ShapeDtypeStruct((b, l, hv * LANES), BF16),
                   jax.ShapeDtypeStruct((b, l, hv * LANES), BF16),
                   jax.ShapeDtypeStruct(s0g.shape, F32)],
        scratch_shapes=[pltpu.VMEM((2 * hb * rep, LANES, LANES), F32),
                        pltpu.VMEM((2, 2, LANES, c), F32)],
        compiler_params=_cparams("parallel", "parallel", "arbitrary"),
        name="gdn_core",
    )(qkv, qkv, qkv, ab, qkv, qkv, qkv, ab, alog_row, dtb_row, s0g)
    sg = sg.reshape(b, ngrp, 2, hb * rep, LANES, LANES).transpose(0, 2, 1, 3, 4, 5)
    return of, ob, sg.reshape(b, 2, hv, LANES, LANES)


def _rotary(x, tabs):
    c_ref, sa_ref, sb_ref = tabs
    return (x * c_ref[...] + pltpu.roll(x, 96, 1) * sa_ref[...]
            + pltpu.roll(x, 32, 1) * sb_ref[...])


def _ret_core_kernel(*refs, hp, rope):
    qf_ref, kf_ref, vf_ref, qb_ref, kb_ref, vb_ref = refs[:6]
    tabs_f, tabs_b = (refs[6:9], refs[9:12]) if rope else (None, None)
    lg_ref, s0_ref, of_ref, ob_ref, sout_ref, s_ref, mask_ref, dec_ref = refs[12 if rope else 6:]
    t = pl.program_id(2)
    c = qf_ref.shape[1]
    dk, dv = RET_QK_DIM, RET_V_DIM
    scale = dk ** -0.5

    @pl.when(t == 0)
    def _():
        rr = lax.broadcasted_iota(jnp.int32, (c, c), 0)
        cc = lax.broadcasted_iota(jnp.int32, (c, c), 1)
        diff = (rr - cc).astype(F32)
        pos = lax.broadcasted_iota(jnp.int32, (c, LANES), 0).astype(F32)
        for h in range(hp):
            lgf, lgb = lg_ref[h, 0:1, :], lg_ref[h, 1:2, :]
            s_ref[2 * h] = s0_ref[0, h, 0]
            s_ref[2 * h + 1] = s0_ref[0, h, 1]
            mask_ref[h] = scale * jnp.where(diff > 0, jnp.exp(diff * lgf[:, :c]),
                                            jnp.where(diff < 0, jnp.exp(-diff * lgb[:, :c]), 2.0))
            lgf1, lgb1 = lgf[:, :LANES], lgb[:, :LANES]
            dec_ref[4 * h] = scale * jnp.exp((pos + 1.0) * lgf1)
            dec_ref[4 * h + 1] = jnp.exp((c - 1.0 - pos) * lgf1)
            dec_ref[4 * h + 2] = scale * jnp.exp((c - pos) * lgb1)
            dec_ref[4 * h + 3] = jnp.exp(pos * lgb1)

    def load(ref, h, tabs):
        x = ref[0, :, h * dk:(h + 1) * dk].astype(F32)
        return _rotary(x, tabs) if rope else x

    heads = []
    for h in range(hp):
        q, k = load(qf_ref, h, tabs_f), load(kf_ref, h, tabs_f)
        qb, kb = load(qb_ref, h, tabs_b), load(kb_ref, h, tabs_b)
        v, vb = vf_ref[0, :, h * dv:(h + 1) * dv], vb_ref[0, :, h * dv:(h + 1) * dv]
        sf, sb = s_ref[2 * h], s_ref[2 * h + 1]
        heads.append((v, sf, sb,
                      _dot_nt(q, k),
                      _dot(q * dec_ref[4 * h], sf),
                      _dot((k * dec_ref[4 * h + 1]).T, v),
                      _dot(qb * dec_ref[4 * h + 2], sb),
                      _dot((kb * dec_ref[4 * h + 3]).T, vb)))
    for h, (v, sf, sb, sc, inter_f, kv_f, inter_b, kv_b) in enumerate(heads):
        sl = slice(h * dv, (h + 1) * dv)
        of_ref[0, :, sl] = (_dot(sc * mask_ref[h], v) + inter_f).astype(of_ref.dtype)
        ob_ref[0, :, sl] = inter_b.astype(ob_ref.dtype)
        s_ref[2 * h] = sf * jnp.exp(c * lg_ref[h, 0:1, :]) + kv_f
        s_ref[2 * h + 1] = sb * jnp.exp(c * lg_ref[h, 1:2, :]) + kv_b

    @pl.when(t == pl.num_programs(2) - 1)
    def _():
        for h in range(hp):
            sout_ref[0, h, 0] = s_ref[2 * h]
            sout_ref[0, h, 1] = s_ref[2 * h + 1]


def _ret_core(pre, tables, lg, s0, hp=RET_HEADS_PER_STEP):
    b, l, _ = pre.shape
    c = min(RET_CHUNK, l)
    nt = l // c
    h_, dk, dv = RET_HEADS, RET_QK_DIM, RET_V_DIM
    ng = h_ // hp
    fwd = lambda off: (lambda bi, g, t: (bi, t, off + g))
    bwd = lambda off: (lambda bi, g, t: (bi, nt - 1 - t, off + g))

    def specs(mk):
        return [pl.BlockSpec((1, c, hp * dk), mk(0)), pl.BlockSpec((1, c, hp * dk), mk(ng)),
                pl.BlockSpec((1, c, hp * dv), mk(ng))]

    rope = tables is not None
    tab_specs, tab_args = [], []
    if rope:
        tab_specs = ([pl.BlockSpec((c, LANES), lambda bi, g, t: (t, 0))] * 3
                     + [pl.BlockSpec((c, LANES), lambda bi, g, t: (nt - 1 - t, 0))] * 3)
        tab_args = list(tables) * 2
    st = pl.BlockSpec((1, hp, 2, dk, dv), lambda bi, g, t: (bi, g, 0, 0, 0))
    return pl.pallas_call(
        functools.partial(_ret_core_kernel, hp=hp, rope=rope),
        grid=(b, ng, nt),
        in_specs=(specs(fwd) + specs(bwd) + tab_specs
                  + [pl.BlockSpec((hp, 2, dv), lambda bi, g, t: (g, 0, 0)), st]),
        out_specs=[pl.BlockSpec((1, c, hp * dv), fwd(0)), pl.BlockSpec((1, c, hp * dv), bwd(0)), st],
        out_shape=[jax.ShapeDtypeStruct((b, l, h_ * dv), BF16),
                   jax.ShapeDtypeStruct((b, l, h_ * dv), BF16),
                   jax.ShapeDtypeStruct((b, h_, 2, dk, dv), F32)],
        scratch_shapes=[pltpu.VMEM((2 * hp, dk, dv), F32), pltpu.VMEM((hp, c, c), F32),
                        pltpu.VMEM((4 * hp, c, LANES), F32)],
        compiler_params=_cparams("parallel", "parallel", "arbitrary"),
        name="retention_core",
    )(pre, pre, pre, pre, pre, pre, *tab_args, lg, s0)


def _attention_layer(x, xc, m, mc, norm_g, w_in, q_g, k_g, w_out, tables, want_ctx):
    hd = ATT_HEADS * ATT_HEAD_DIM
    kvd = ATT_KV_HEADS * ATT_HEAD_DIM
    w_in = w_in.astype(BF16)
    w_out = w_out.astype(BF16)
    gains = jnp.concatenate([jnp.tile(q_g * (ATT_HEAD_DIM ** -0.5 * LOG2E), (ATT_HEADS, 1)),
                             jnp.tile(k_g, (ATT_KV_HEADS, 1))], axis=0)

    def project(t, mm, tabs):
        qkv = _proj(t, norm_g, mm[1], mm[0], w_in, BF16, tm=1024, tn=hd + 2 * kvd)
        qk = _head_prep(qkv, gains, tabs, norm=True)
        return qk, jnp.concatenate([qk[:, :, hd:], qkv[:, :, hd + kvd:]], axis=-1)

    qk, kv = project(x, m, tables)
    qk_c, kv_c = project(xc, mc, None)
    o = _attention(qk, jnp.concatenate([kv, kv_c], axis=1))
    tm = min(512, x.shape[1])
    x = _out_res(_plain_prologue, [o], [_row_spec(tm, hd)], w_out, x, m[2], tm)
    if want_ctx:
        oc = _attention(qk_c, kv_c)
        tmc = min(512, xc.shape[1])
        xc = _out_res(_plain_prologue, [oc], [_row_spec(tmc, hd)], w_out, xc, mc[2], tmc)
    return x, xc


def _gdn_layer(x, xc, m, mc, norm_g, w_in, conv_w, a_log, dt_bias, gn_g, w_out, want_ctx):
    hq, hv, dh = GDN_QK_HEADS, GDN_V_HEADS, GDN_HEAD_DIM
    cw = (2 * hq + hv) * dh
    zw = hv * dh
    w_qkv = w_in[:, :cw].astype(BF16)
    w_z = w_in[:, cw:cw + zw].astype(BF16)
    w_ab = jnp.pad(w_in[:, cw + zw:], ((0, 0), (0, LANES - 4 * hv))).astype(BF16)
    w_out = w_out.astype(BF16)
    pad = (0, LANES - 2 * hv)
    alog_row = jnp.pad(a_log.reshape(-1), pad).reshape(1, LANES)
    dtb_row = jnp.pad(dt_bias.reshape(-1), pad).reshape(1, LANES)

    def project(t, mm):
        return _gdn_in(t, norm_g, mm[1], mm[0], w_qkv, w_z, w_ab, conv_w, n_norm=2 * hq)

    b = x.shape[0]
    qkv_c, z_c, ab_c = project(xc, mc)
    qkv, z, ab = project(x, m)
    s0 = jnp.zeros((b, 2, hv, dh, dh), F32)
    ocf, ocb, sc = _gdn_core(qkv_c, ab_c, alog_row, dtb_row, s0)
    of, ob, _ = _gdn_core(qkv, ab, alog_row, dtb_row, sc)

    def out(of_, ob_, z_, t, mm):
        tm = min(512, t.shape[1])
        gspec = pl.BlockSpec((1, LANES), lambda bi, i: (0, 0))
        return _out_res(_gdn_prologue, [of_, ob_, z_, gn_g.reshape(1, dh)],
                        [_row_spec(tm, zw)] * 3 + [gspec], w_out, t, mm[2], tm)

    x = out(of, ob, z, x, m)
    if want_ctx:
        xc = out(ocf, ocb, z_c, xc, mc)
    return x, xc


def _retention_layer(x, xc, m, mc, norm_g, w_in, decay, w_out, tables, want_ctx):
    h_, dk, dv = RET_HEADS, RET_QK_DIM, RET_V_DIM
    w_in = w_in.astype(BF16)
    w_out = w_out.astype(BF16)
    lg = jnp.broadcast_to(-decay.astype(F32).T[:, :, None], (h_, 2, dv))

    def project(t, mm):
        return _proj(t, norm_g, mm[1], mm[0], w_in, BF16, tm=1024, tn=2048)

    b = x.shape[0]
    vg_c = project(xc, mc)
    vg = project(x, m)
    s0 = jnp.zeros((b, h_, 2, dk, dv), F32)
    ocf, ocb, sc = _ret_core(vg_c, None, lg, s0)
    of, ob, _ = _ret_core(vg, tables, lg, sc)
    vw = h_ * dv

    def out(of_, ob_, vg_, t, mm):
        tm = min(512, t.shape[1])
        return _out_res(_ret_prologue, [of_, ob_, vg_],
                        [_row_spec(tm, vw), _row_spec(tm, vw), _row_spec(tm, vw, col=2)],
                        w_out, t, mm[2], tm)

    x = out(of, ob, vg, x, m)
    if want_ctx:
        xc = out(ocf, ocb, vg_c, xc, mc)
    return x, xc


def _ffn(x, m, norm_g, w_in, conv_w, conv_b, w_out, final_g=None):
    return _convglu(x, norm_g, m[4], m[3], w_in.astype(BF16), conv_w, conv_b, w_out.astype(BF16),
                    m[5], final_g)


def kernel(x, c, ctx, c_ctx, mod_w, mod_b, norm1_g, norm2_g, att_w_in, att_q_g, att_k_g, att_w_out, gdn_w_in, gdn_conv_w, gdn_A_log, gdn_dt_bias, gdn_norm_g, gdn_w_out, ret_w_in, ret_decay, ret_w_out, ffn_w_in, ffn_conv_w, ffn_conv_b, ffn_w_out, final_g):
    b, n, d = x.shape
    depth = mod_w.shape[0]
    cc = jnp.concatenate([c, c_ctx[None, :], jnp.zeros((8 - b - 1, d), F32)], axis=0)
    mods = _modulation(cc, mod_w, mod_b)
    tab_att = _rope_tables(n, ATT_HEAD_DIM)
    tab_ret = _rope_tables(n, RET_QK_DIM)
    xc = ctx
    for i in range(depth):
        last = i == depth - 1
        mi = mods[i].reshape(8, 6, d)
        m = [mi[:b, s][:, None, :] for s in range(6)]
        mc = [jnp.broadcast_to(mi[b, s][None, None, :], (b, 1, d)) for s in range(6)]
        kind, j = i % N_MIXERS, i // N_MIXERS
        if kind == 0:
            x, xc = _attention_layer(x, xc, m, mc, norm1_g[i], att_w_in[j], att_q_g[j], att_k_g[j],
                                     att_w_out[j], tab_att, not last)
        elif kind == 1:
            x, xc = _gdn_layer(x, xc, m, mc, norm1_g[i], gdn_w_in[j], gdn_conv_w[j], gdn_A_log[j],
                               gdn_dt_bias[j], gdn_norm_g[j], gdn_w_out[j], not last)
        else:
            x, xc = _retention_layer(x, xc, m, mc, norm1_g[i], ret_w_in[j], ret_decay[j],
                                     ret_w_out[j], tab_ret, not last)
        x = _ffn(x, m, norm2_g[i], ffn_w_in[i], ffn_conv_w[i], ffn_conv_b[i], ffn_w_out[i],
                 final_g if last else None)
        if not last:
            xc = _ffn(xc, mc, norm2_g[i], ffn_w_in[i], ffn_conv_w[i], ffn_conv_b[i], ffn_w_out[i])
    return x
```

```python
import functools

import jax
import jax.numpy as jnp
from jax import lax
from jax.experimental import pallas as pl
from jax.experimental.pallas import tpu as pltpu

F32 = jnp.float32
BF16 = jnp.bfloat16

EPS = 1e-6
LOG2E = 1.4426950408889634
ROPE_THETA = 10000.0
GRID_W = 64
N_MIXERS = 3

ATT_HEADS, ATT_KV_HEADS, ATT_HEAD_DIM = 8, 2, 128
GDN_QK_HEADS, GDN_V_HEADS, GDN_HEAD_DIM, GDN_CONV_W = 8, 16, 128, 5
RET_HEADS, RET_QK_DIM, RET_V_DIM = 8, 128, 256
FFN_CONV_W = 3

LANES = 128
BF16_ROWS = 16
MXU_DIM = 256
GDN_CHUNK = 128
GDN_INV_BLOCK = 16
GDN_HEADS_PER_STEP = 8
RET_CHUNK = 256
RET_HEADS_PER_STEP = 8
ATT_MAX_UNROLL = 11
VMEM_LIMIT = 56 * 1024 * 1024


def _cparams(*sem):
    return pltpu.CompilerParams(dimension_semantics=sem, vmem_limit_bytes=VMEM_LIMIT)


def _sigmoid(x):
    return 1.0 / (1.0 + jnp.exp(-x))


def _silu(x):
    return x * _sigmoid(x)


def _dot(a, b):
    return jnp.dot(a.astype(BF16), b.astype(BF16), preferred_element_type=F32)


def _dot_nt(a, b):
    return lax.dot_general(a.astype(BF16), b.astype(BF16), (((1,), (1,)), ((), ())),
                           preferred_element_type=F32)


def _mod_kernel(cc_ref, w_ref, b_ref, o_ref):
    s = _silu(cc_ref[...])
    o_ref[0] = _dot(s, w_ref[0]) + b_ref[0]


def _modulation(cc, mod_w, mod_b):
    depth, d, n6 = mod_w.shape
    tn = n6 // 4
    return pl.pallas_call(
        _mod_kernel,
        grid=(depth, n6 // tn),
        in_specs=[pl.BlockSpec((8, d), lambda l, j: (0, 0)),
                  pl.BlockSpec((1, d, tn), lambda l, j: (l, 0, j)),
                  pl.BlockSpec((1, 1, tn), lambda l, j: (l, 0, j))],
        out_specs=pl.BlockSpec((1, 8, tn), lambda l, j: (l, 0, j)),
        out_shape=jax.ShapeDtypeStruct((depth, 8, n6), F32),
        compiler_params=_cparams("parallel", "parallel"),
        name="modulation",
    )(cc, mod_w, mod_b.reshape(depth, 1, n6))


def _norm_mod(x, g, sc, sh):
    ms = jnp.mean(x * x, axis=-1, keepdims=True)
    return (x * lax.rsqrt(ms + EPS)) * g * (1.0 + sc) + sh


def _proj_kernel(x_ref, g_ref, sc_ref, sh_ref, w_ref, o_ref, h_ref):
    @pl.when(pl.program_id(2) == 0)
    def _():
        h_ref[...] = _norm_mod(x_ref[0], g_ref[...], sc_ref[0], sh_ref[0]).astype(BF16)

    o_ref[0] = jnp.dot(h_ref[...], w_ref[...], preferred_element_type=F32).astype(o_ref.dtype)


def _proj(x, g, sc, sh, w, out_dtype, tm, tn):
    b, l, d = x.shape
    n = w.shape[1]
    tm = min(tm, l)
    tn = min(tn, n)
    return pl.pallas_call(
        _proj_kernel,
        grid=(b, l // tm, n // tn),
        in_specs=[pl.BlockSpec((1, tm, d), lambda bi, i, j: (bi, i, 0)),
                  pl.BlockSpec((1, d), lambda bi, i, j: (0, 0)),
                  pl.BlockSpec((1, 1, d), lambda bi, i, j: (bi, 0, 0)),
                  pl.BlockSpec((1, 1, d), lambda bi, i, j: (bi, 0, 0)),
                  pl.BlockSpec((d, tn), lambda bi, i, j: (0, j))],
        out_specs=pl.BlockSpec((1, tm, tn), lambda bi, i, j: (bi, i, j)),
        out_shape=jax.ShapeDtypeStruct((b, l, n), out_dtype),
        scratch_shapes=[pltpu.VMEM((tm, d), BF16)],
        compiler_params=_cparams("parallel", "parallel", "arbitrary"),
        name="norm_mod_proj",
    )(x, g.reshape(1, d), sc, sh, w)


def _rope_tables(n, head_dim):
    rows = n // GRID_W
    row = jnp.repeat(jnp.arange(rows, dtype=F32), GRID_W)
    col = jnp.tile(jnp.arange(GRID_W, dtype=F32), rows)
    axis_dim = head_dim // 2
    inv_freq = ROPE_THETA ** (-jnp.arange(0, axis_dim, 2, dtype=F32) / axis_dim)
    ar, ac = row[:, None] * inv_freq, col[:, None] * inv_freq
    cr, sr, cc, sc = jnp.cos(ar), jnp.sin(ar), jnp.cos(ac), jnp.sin(ac)
    z = jnp.zeros_like(sr)
    return (jnp.concatenate([cr, cr, cc, cc], axis=-1),
            jnp.concatenate([-sr, z, -sc, z], axis=-1),
            jnp.concatenate([z, sr, z, sc], axis=-1))


def _head_prep_kernel(*refs, norm, rope):
    if rope:
        x_ref, g_ref, c_ref, sa_ref, sb_ref, o_ref = refs
    else:
        x_ref, g_ref, o_ref = refs
    for h in range(g_ref.shape[1]):
        sl = slice(h * LANES, (h + 1) * LANES)
        x = x_ref[0, :, sl].astype(F32)
        if norm:
            x = x * lax.rsqrt(jnp.mean(x * x, axis=-1, keepdims=True) + EPS)
        x = x * g_ref[0, h:h + 1, :]
        if rope:
            x = (x * c_ref[...] + pltpu.roll(x, 96, 1) * sa_ref[...]
                 + pltpu.roll(x, 32, 1) * sb_ref[...])
        o_ref[0, :, sl] = x.astype(o_ref.dtype)


def _head_prep(x, gains, tables, norm, tm=512):
    b, l, _ = x.shape
    nb = gains.shape[0]
    tm = min(tm, l)
    hw = max(h for h in (8, 5, 4, 2, 1) if nb % h == 0)
    in_specs = [pl.BlockSpec((1, tm, hw * LANES), lambda bi, i, c: (bi, i, c)),
                pl.BlockSpec((1, hw, LANES), lambda bi, i, c: (c, 0, 0))]
    args = [x, gains.reshape(nb // hw, hw, LANES)]
    if tables is not None:
        in_specs += [pl.BlockSpec((tm, LANES), lambda bi, i, c: (i, 0))] * 3
        args += list(tables)
    return pl.pallas_call(
        functools.partial(_head_prep_kernel, norm=norm, rope=tables is not None),
        grid=(b, l // tm, nb // hw),
        in_specs=in_specs,
        out_specs=pl.BlockSpec((1, tm, hw * LANES), lambda bi, i, c: (bi, i, c)),
        out_shape=jax.ShapeDtypeStruct((b, l, nb * LANES), BF16),
        compiler_params=_cparams("parallel", "parallel", "parallel"),
        name="head_prep",
    )(*args)


def _attn_kernel(q_ref, k_ref, vt_ref, o_ref, *, groups, tq, unroll):
    dh = LANES
    qt = jnp.concatenate([q_ref[0, :, g * dh:(g + 1) * dh].astype(F32).T for g in range(groups)],
                         axis=1).astype(BF16)
    rows = groups * tq
    nk, vrows, tk = vt_ref.shape[2:]

    def scores(j):
        off = pl.multiple_of(j * tk, tk)
        return jnp.dot(k_ref[0, pl.ds(off, tk), :], qt, preferred_element_type=F32)

    def body(it, carry):
        m, acc = carry
        base = it * unroll
        st = scores(base)
        pv = alpha = None
        for u in range(unroll):
            st_next = scores(base + u + 1) if u + 1 < unroll else None
            m_new = jnp.maximum(m, jnp.max(st, axis=0, keepdims=True))
            p = jnp.exp2(st - m_new).astype(BF16)
            if pv is not None:
                acc = alpha * acc + pv
            alpha = jnp.exp2(m - m_new)
            pv = jnp.dot(vt_ref[0, 0, base + u], p, preferred_element_type=F32)
            m, st = m_new, st_next
        return m, alpha * acc + pv

    init = (jnp.full((1, rows), -1e30, F32), jnp.zeros((vrows, rows), F32))
    _, acc = lax.fori_loop(0, nk // unroll, body, init)
    out = acc[:dh] / acc[dh:dh + 1]
    for g in range(groups):
        o_ref[0, :, g * dh:(g + 1) * dh] = out[:, g * tq:(g + 1) * tq].T.astype(o_ref.dtype)


def _attention(q, kv, tq=1024, tk=256):
    b, n, _ = q.shape
    lk = kv.shape[1]
    kvh = ATT_KV_HEADS
    groups = ATT_HEADS // kvh
    hd = ATT_HEADS * LANES
    gw = groups * LANES
    tq = min(tq, n)
    tk = min(tk, lk)
    nk = lk // tk
    vt = kv[:, :, kvh * LANES:].reshape(b, nk, tk, kvh, LANES).transpose(0, 3, 1, 4, 2)
    vt = jnp.concatenate([vt, jnp.ones((b, kvh, nk, BF16_ROWS, tk), BF16)], axis=3)
    vrows = LANES + BF16_ROWS
    unroll = max(u for u in range(1, ATT_MAX_UNROLL + 1) if nk % u == 0)
    return pl.pallas_call(
        functools.partial(_attn_kernel, groups=groups, tq=tq, unroll=unroll),
        grid=(b, kvh, n // tq),
        in_specs=[pl.BlockSpec((1, tq, gw), lambda bi, h, i: (bi, i, h)),
                  pl.BlockSpec((1, lk, LANES), lambda bi, h, i: (bi, 0, h)),
                  pl.BlockSpec((1, 1, nk, vrows, tk), lambda bi, h, i: (bi, h, 0, 0, 0))],
        out_specs=pl.BlockSpec((1, tq, gw), lambda bi, h, i: (bi, i, h)),
        out_shape=jax.ShapeDtypeStruct((b, n, hd), BF16),
        compiler_params=_cparams("parallel", "parallel", "parallel"),
        name="gqa_attention",
    )(q, kv, vt)


def _plain_prologue(j, a_ref):
    return a_ref[0, :, j * MXU_DIM:(j + 1) * MXU_DIM]


def _gdn_prologue(j, of_ref, ob_ref, z_ref, g_ref):
    outs = []
    g = g_ref[...]
    for h in range(j * MXU_DIM // LANES, (j + 1) * MXU_DIM // LANES):
        sl = slice(h * LANES, (h + 1) * LANES)
        o = of_ref[0, :, sl].astype(F32) + ob_ref[0, :, sl].astype(F32)
        o = o * lax.rsqrt(jnp.mean(o * o, axis=-1, keepdims=True) + EPS) * g
        outs.append((o * _silu(z_ref[0, :, sl].astype(F32))).astype(BF16))
    return jnp.concatenate(outs, axis=-1)


def _ret_prologue(j, of_ref, ob_ref, gate_ref):
    sl = slice(j * RET_V_DIM, (j + 1) * RET_V_DIM)
    o = of_ref[0, :, sl].astype(F32) + ob_ref[0, :, sl].astype(F32)
    o = o * lax.rsqrt(jnp.mean(o * o, axis=-1, keepdims=True) + EPS)
    return (o * _silu(gate_ref[0, :, sl].astype(F32))).astype(BF16)


def _out_res_kernel(*refs, prologue, n_in):
    w_ref, x_ref, ga_ref, o_ref = refs[n_in:]
    ins = refs[:n_in]
    n = w_ref.shape[0] // MXU_DIM
    a, y = prologue(0, *ins), None
    for j in range(n):
        a_next = prologue(j + 1, *ins) if j + 1 < n else None
        part = jnp.dot(a, w_ref[j * MXU_DIM:(j + 1) * MXU_DIM, :], preferred_element_type=F32)
        y = part if y is None else y + part
        a = a_next
    o_ref[0] = x_ref[0] + ga_ref[0] * y


def _out_res(prologue, ins, in_specs, w, x, ga, tm):
    b, l, d = x.shape
    k = w.shape[0]
    return pl.pallas_call(
        functools.partial(_out_res_kernel, prologue=prologue, n_in=len(ins)),
        grid=(b, l // tm),
        in_specs=in_specs + [pl.BlockSpec((k, d), lambda bi, i: (0, 0)),
                             pl.BlockSpec((1, tm, d), lambda bi, i: (bi, i, 0)),
                             pl.BlockSpec((1, 1, d), lambda bi, i: (bi, 0, 0))],
        out_specs=pl.BlockSpec((1, tm, d), lambda bi, i: (bi, i, 0)),
        out_shape=jax.ShapeDtypeStruct((b, l, d), F32),
        compiler_params=_cparams("parallel", "parallel"),
        name="out_proj_residual",
    )(*ins, w, x, ga)


def _row_spec(tm, width, col=0):
    return pl.BlockSpec((1, tm, width), lambda bi, i: (bi, i, col))


def _convglu_kernel(x_ref, xp_ref, xn_ref, g_ref, sc_ref, sh_ref, win_ref, cw_ref, cb_ref, wout_ref,
                    ga_ref, fg_ref, o_ref, *, sub, final):
    i = pl.program_id(1)
    tm = x_ref.shape[1]
    f = wout_ref.shape[0]
    hl = xp_ref.shape[1]
    x = x_ref[0]
    xe = jnp.concatenate([xp_ref[0], x, xn_ref[0]], axis=0)
    he = _norm_mod(xe, g_ref[...], sc_ref[0], sh_ref[0]).astype(BF16)
    h = he[hl:hl + tm]
    r = lax.broadcasted_iota(jnp.int32, (tm + 2 * hl, 1), 0)
    inside = jnp.logical_and(jnp.logical_or(r >= hl, i > 0),
                             jnp.logical_or(r < hl + tm, i < pl.num_programs(1) - 1))
    keep = jnp.where(inside, 1.0, 0.0)

    def project(j):
        gate = jnp.dot(he, win_ref[:, f + j * sub:f + (j + 1) * sub],
                       preferred_element_type=F32) * keep
        val = jnp.dot(h, win_ref[:, j * sub:(j + 1) * sub], preferred_element_type=F32)
        return gate, val

    def act(gate, val, j):
        sl = slice(j * sub, (j + 1) * sub)
        cw = cw_ref[:, sl]
        conv = (cw[0:1] * gate[hl - 1:hl - 1 + tm] + cw[1:2] * gate[hl:hl + tm]
                + cw[2:3] * gate[hl + 1:hl + 1 + tm] + cb_ref[:, sl])
        gelu = 0.5 * conv * (1.0 + lax.erf(conv * (2.0 ** -0.5)))
        return (gelu * val).astype(BF16)

    n = f // sub
    gv, y = project(0), None
    for j in range(n):
        gv_next = project(j + 1) if j + 1 < n else None
        a = act(*gv, j)
        part = jnp.dot(a, wout_ref[j * sub:(j + 1) * sub, :], preferred_element_type=F32)
        y = part if y is None else y + part
        gv = gv_next
    out = x + ga_ref[0] * y
    if final:
        out = out * lax.rsqrt(jnp.mean(out * out, axis=-1, keepdims=True) + EPS) * fg_ref[...]
    o_ref[0] = out


def _convglu(x, g, sc, sh, w_in, conv_w, conv_b, w_out, ga, final_g=None, tm=512):
    b, l, d = x.shape
    f = w_out.shape[0]
    tm = min(tm, l)
    sub = MXU_DIM if f % MXU_DIM == 0 else LANES
    hl = BF16_ROWS
    hb = tm // hl
    last_hb = l // hl - 1
    final = final_g is not None
    fg = (final_g if final else jnp.ones((d,), F32)).reshape(1, d)
    const = lambda shape: pl.BlockSpec(shape, lambda bi, i: (0,) * len(shape))
    mod = pl.BlockSpec((1, 1, d), lambda bi, i: (bi, 0, 0))
    return pl.pallas_call(
        functools.partial(_convglu_kernel, sub=sub, final=final),
        grid=(b, l // tm),
        in_specs=[pl.BlockSpec((1, tm, d), lambda bi, i: (bi, i, 0)),
                  pl.BlockSpec((1, hl, d), lambda bi, i: (bi, jnp.maximum(i * hb - 1, 0), 0)),
                  pl.BlockSpec((1, hl, d), lambda bi, i: (bi, jnp.minimum((i + 1) * hb, last_hb), 0)),
                  const((1, d)), mod, mod, const((d, 2 * f)), const((FFN_CONV_W, f)),
                  const((1, f)), const((f, d)), mod, const((1, d))],
        out_specs=pl.BlockSpec((1, tm, d), lambda bi, i: (bi, i, 0)),
        out_shape=jax.ShapeDtypeStruct((b, l, d), F32),
        compiler_params=_cparams("parallel", "parallel"),
        name="convglu",
    )(x, x, x, g.reshape(1, d), sc, sh, w_in, conv_w, conv_b.reshape(1, f), w_out, ga, fg)


def _gdn_in_kernel(x_ref, xp_ref, xn_ref, g_ref, sc_ref, sh_ref, w_ref, wz_ref, wab_ref, cw_ref,
                   qkv_ref, z_ref, ab_ref, *, n_norm, sub):
    i = pl.program_id(1)
    tm = x_ref.shape[1]
    hl = xp_ref.shape[1]
    ch = cw_ref.shape[1]
    pad = GDN_CONV_W // 2
    xe = jnp.concatenate([xp_ref[0], x_ref[0], xn_ref[0]], axis=0)
    he = _norm_mod(xe, g_ref[...], sc_ref[0], sh_ref[0]).astype(BF16)
    h = he[hl:hl + tm]
    r = lax.broadcasted_iota(jnp.int32, (tm + 2 * hl, 1), 0)
    inside = jnp.logical_and(jnp.logical_or(r >= hl, i > 0),
                             jnp.logical_or(r < hl + tm, i < pl.num_programs(1) - 1))
    keep = jnp.where(inside, 1.0, 0.0)

    def project(j):
        return jnp.dot(he, w_ref[:, j * sub:(j + 1) * sub], preferred_element_type=F32) * keep

    def act(pre, j):
        for hh in range(sub // LANES):
            lo = j * sub + hh * LANES
            cw = cw_ref[:, lo:lo + LANES]
            src = pre[:, hh * LANES:(hh + 1) * LANES]
            y = None
            for t in range(GDN_CONV_W):
                tap = src if t == pad else pltpu.roll(src, (pad - t) % src.shape[0], 0)
                term = cw[t:t + 1] * tap[hl:hl + tm]
                y = term if y is None else y + term
            y = _silu(y)
            if lo // LANES < n_norm:
                y = y * lax.rsqrt(jnp.sum(y * y, axis=-1, keepdims=True) + EPS)
            qkv_ref[0, :, lo:lo + LANES] = y.astype(qkv_ref.dtype)

    n = ch // sub
    pre = project(0)
    for j in range(n):
        pre_next = project(j + 1) if j + 1 < n else None
        if j == n - 1:
            z_ref[0] = jnp.dot(h, wz_ref[...], preferred_element_type=F32).astype(z_ref.dtype)
            ab_ref[0] = jnp.dot(h, wab_ref[...], preferred_element_type=F32)
        act(pre, j)
        pre = pre_next


def _gdn_in(x, g, sc, sh, w_qkv, w_z, w_ab, conv_w, n_norm, tm=512):
    b, l, d = x.shape
    ch, zw = w_qkv.shape[1], w_z.shape[1]
    tm = min(tm, l)
    hl = BF16_ROWS
    hb = tm // hl
    last_hb = l // hl - 1
    const = lambda shape: pl.BlockSpec(shape, lambda bi, i: (0,) * len(shape))
    mod = pl.BlockSpec((1, 1, d), lambda bi, i: (bi, 0, 0))
    rows = lambda w: pl.BlockSpec((1, tm, w), lambda bi, i: (bi, i, 0))
    return pl.pallas_call(
        functools.partial(_gdn_in_kernel, n_norm=n_norm, sub=MXU_DIM),
        grid=(b, l // tm),
        in_specs=[rows(d),
                  pl.BlockSpec((1, hl, d), lambda bi, i: (bi, jnp.maximum(i * hb - 1, 0), 0)),
                  pl.BlockSpec((1, hl, d), lambda bi, i: (bi, jnp.minimum((i + 1) * hb, last_hb), 0)),
                  const((1, d)), mod, mod, const((d, ch)), const((d, zw)), const((d, LANES)),
                  const((GDN_CONV_W, ch))],
        out_specs=[rows(ch), rows(zw), rows(LANES)],
        out_shape=[jax.ShapeDtypeStruct((b, l, ch), BF16), jax.ShapeDtypeStruct((b, l, zw), BF16),
                   jax.ShapeDtypeStruct((b, l, LANES), F32)],
        compiler_params=_cparams("parallel", "parallel"),
        name="gdn_in",
    )(x, x, x, g.reshape(1, d), sc, sh, w_qkv, w_z, w_ab, conv_w)


def _mm(xs, ys):
    return [jnp.dot(x, y, preferred_element_type=F32) for x, y in zip(xs, ys)]


def _bf(xs):
    return [x.astype(BF16) for x in xs]


def _neumann_minus_eye(ms, order):
    mb = _bf(ms)
    m2 = _mm(mb, mb)
    pb = _bf(m2)
    m3 = _mm(mb, pb)
    r = [b - a - c for a, b, c in zip(ms, m2, m3)]
    p, mp = 4, m2
    while p < order:
        mp = _mm(pb, pb)
        pb = _bf(mp)
        rm = _mm(_bf(r), pb)
        r = [x + y + z for x, y, z in zip(r, mp, rm)]
        p *= 2
    return r


def _unit_tri_inverse_minus_eye(a_list, same_block):
    c = a_list[0].shape[0]
    d = [jnp.where(same_block, a, 0.0) for a in a_list]
    lo = [a - x for a, x in zip(a_list, d)]
    r0 = _neumann_minus_eye(d, GDN_INV_BLOCK)
    r0b = _bf(r0)
    n = [x + y for x, y in zip(lo, _mm(r0b, _bf(lo)))]
    rn = _neumann_minus_eye(n, c // GDN_INV_BLOCK)
    return [x + y + z for x, y, z in zip(r0, rn, _mm(_bf(rn), r0b))]


def _gdn_core_kernel(qf_ref, kf_ref, vf_ref, abf_ref, qb_ref, kb_ref, vb_ref, abb_ref,
                     alog_ref, dtb_ref, s0_ref, of_ref, ob_ref, sout_ref, s_ref, gt_ref, *, hb):
    t = pl.program_id(2)
    grp = pl.program_id(1)
    rep = GDN_V_HEADS // GDN_QK_HEADS
    c, dk = GDN_CHUNK, GDN_HEAD_DIM
    scale = dk ** -0.5

    @pl.when(t == 0)
    def _():
        s_ref[...] = s0_ref[0]

    rr = lax.broadcasted_iota(jnp.int32, (c, c), 0)
    cc = lax.broadcasted_iota(jnp.int32, (c, c), 1)
    same_block = (rr // GDN_INV_BLOCK) == (cc // GDN_INV_BLOCK)
    lane = lax.broadcasted_iota(jnp.int32, (c, LANES), 1)
    ng = 2 * GDN_V_HEADS
    dirs = ((qf_ref, kf_ref, vf_ref, abf_ref, of_ref, rr >= cc, rr > cc),
            (qb_ref, kb_ref, vb_ref, abb_ref, ob_ref, rr <= cc, rr < cc))

    gcs, balls, kks, qks, kts = [], [], [], [], []
    for q_ref, k_ref, _, ab_ref, _, incl, _ in dirs:
        ab = ab_ref[0]
        sp = ab + dtb_ref[...]
        sp = jnp.maximum(sp, 0.0) + jnp.log1p(jnp.exp(-jnp.abs(sp)))
        gall = -jnp.exp(alog_ref[...]) * sp
        balls.append(_sigmoid(ab))
        g1 = gall.astype(BF16)
        r1 = gall - g1.astype(F32)
        g2 = r1.astype(BF16)
        g3 = (r1 - g2.astype(F32)).astype(BF16)
        packed = jnp.where(lane < ng, g1.astype(F32),
                           jnp.where(lane < 2 * ng, pltpu.roll(g2.astype(F32), ng, 1),
                                     jnp.where(lane < 3 * ng,
                                               pltpu.roll(g3.astype(F32), 2 * ng, 1), 0.0)))
        sel = jnp.concatenate([jnp.where(incl, 1.0, 0.0), jnp.ones((c, c), F32)], axis=0)
        cs = _dot(sel, packed)
        cs = cs + pltpu.roll(cs, LANES - ng, 1) + pltpu.roll(cs, LANES - 2 * ng, 1)
        gcs.append(cs[:c])
        d = len(gcs) - 1
        gt_ref[d, 0] = cs[:c].T
        gt_ref[d, 1] = cs[c:].T
        for hh in range(hb):
            sl = slice(hh * dk, (hh + 1) * dk)
            k = k_ref[0, :, sl]
            g = _dot_nt(jnp.concatenate([k, q_ref[0, :, sl]], axis=0), k)
            kks.append(g[:c])
            qks.append(g[c:])
            kts.append(k.astype(F32).T)

    chains = [(d, hh, e) for d in range(2) for hh in range(hb) for e in range(rep)]
    a_list, xs, qkms, qds, kdts, gls = [], [], [], [], [], []

    def prepare(d, hh, e):
        q_ref, k_ref, v_ref, _, _, incl, strict = dirs[d]
        col = d * GDN_V_HEADS + (grp * hb + hh) * rep + e
        pick = lambda m, j: jnp.sum(jnp.where(lane == j, m, 0.0), axis=-1, keepdims=True)
        gc = pick(gcs[d], col)
        bcol = pick(balls[d], col + ng)
        gc_row = gt_ref[d, 0, pl.ds(col, 1), :]
        gtot_row = gt_ref[d, 1, pl.ds(col, 1), :]
        decay = jnp.where(incl, jnp.exp(jnp.minimum(gc - gc_row, 0.0)), 0.0)
        a_list.append(jnp.where(strict, bcol * kks[d * hb + hh] * decay, 0.0))
        kf = k_ref[0, :, hh * dk:(hh + 1) * dk].astype(F32)
        qf = q_ref[0, :, hh * dk:(hh + 1) * dk].astype(F32)
        vf = v_ref[0, :, (hh * rep + e) * dk:(hh * rep + e + 1) * dk].astype(F32)
        egc = jnp.exp(gc)
        xs.append(jnp.concatenate([vf * bcol, kf * (bcol * egc)], axis=1))
        qkms.append((qks[d * hb + hh] * decay * scale).astype(BF16))
        qds.append(qf * (egc * scale))
        kdts.append((kts[d * hb + hh] * jnp.exp(gtot_row - gc_row)).astype(BF16))
        gls.append(jnp.exp(gtot_row[:, :dk]))

    for ch in chains:
        prepare(*ch)

    tms = _unit_tri_inverse_minus_eye(a_list, same_block)
    uws = [x + y for x, y in zip(xs, _mm(_bf(tms), _bf(xs)))]

    ss = [s_ref[i] for i in range(len(chains))]
    sbs = _bf(ss)
    wss = _mm([jnp.concatenate([uw[:, dk:], qd], axis=0).astype(BF16) for uw, qd in zip(uws, qds)],
              sbs)
    vns = [(uw[:, :dk] - ws[:c]).astype(BF16) for uw, ws in zip(uws, wss)]
    ovs = _mm(qkms, vns)
    kvs = _mm(kdts, vns)
    for i, (d, hh, e) in enumerate(chains):
        o_ref = dirs[d][4]
        o_ref[0, :, (hh * rep + e) * dk:(hh * rep + e + 1) * dk] = (
            wss[i][c:] + ovs[i]).astype(o_ref.dtype)
        s_ref[i] = ss[i] * gls[i] + kvs[i]

    @pl.when(t == pl.num_programs(2) - 1)
    def _():
        sout_ref[0] = s_ref[...]


def _gdn_core(qkv, ab, alog_row, dtb_row, s0, hb=GDN_HEADS_PER_STEP):
    b, l, _ = qkv.shape
    c = GDN_CHUNK
    nt = l // c
    hq, hv = GDN_QK_HEADS, GDN_V_HEADS
    rep = hv // hq
    ngrp = hq // hb
    fwd = lambda off: (lambda bi, g, t: (bi, t, off + g))
    bwd = lambda off: (lambda bi, g, t: (bi, nt - 1 - t, off + g))
    qw, vw = hb * LANES, hb * rep * LANES

    def specs(mk, ab_map):
        return [pl.BlockSpec((1, c, qw), mk(0)),
                pl.BlockSpec((1, c, qw), mk(ngrp)),
                pl.BlockSpec((1, c, vw), mk(ngrp)),
                pl.BlockSpec((1, c, LANES), ab_map)]

    ab_fwd = lambda bi, g, t: (bi, t, 0)
    ab_bwd = lambda bi, g, t: (bi, nt - 1 - t, 0)
    row = pl.BlockSpec((1, LANES), lambda bi, g, t: (0, 0))
    st = pl.BlockSpec((1, 2 * hb * rep, LANES, LANES), lambda bi, g, t: (bi * ngrp + g, 0, 0, 0))
    s0g = s0.reshape(b, 2, ngrp, hb * rep, LANES, LANES).transpose(0, 2, 1, 3, 4, 5)
    s0g = s0g.reshape(b * ngrp, 2 * hb * rep, LANES, LANES)
    of, ob, sg = pl.pallas_call(
        functools.partial(_gdn_core_kernel, hb=hb),
        grid=(b, ngrp, nt),
        in_specs=specs(fwd, ab_fwd) + specs(bwd, ab_bwd) + [row, row, st],
        out_specs=[pl.BlockSpec((1, c, vw), fwd(0)), pl.BlockSpec((1, c, vw), bwd(0)), st],
        out_shape=[jax.ShapeDtypeStruct((b, l, hv * LANES), BF16),
                   jax.ShapeDtypeStruct((b, l, hv * LANES), BF16),
                   jax.ShapeDtypeStruct(s0g.shape, F32)],
        scratch_shapes=[pltpu.VMEM((2 * hb * rep, LANES, LANES), F32),
                        pltpu.VMEM((2, 2, LANES, c), F32)],
        compiler_params=_cparams("parallel", "parallel", "arbitrary"),
        name="gdn_core",
    )(qkv, qkv, qkv, ab, qkv, qkv, qkv, ab, alog_row, dtb_row, s0g)
    sg = sg.reshape(b, ngrp, 2, hb * rep, LANES, LANES).transpose(0, 2, 1, 3, 4, 5)
    return of, ob, sg.reshape(b, 2, hv, LANES, LANES)


def _rotary(x, tabs):
    c_ref, sa_ref, sb_ref = tabs
    return (x * c_ref[...] + pltpu.roll(x, 96, 1) * sa_ref[...]
            + pltpu.roll(x, 32, 1) * sb_ref[...])


def _ret_core_kernel(*refs, hp, rope):
    qf_ref, kf_ref, vf_ref, qb_ref, kb_ref, vb_ref = refs[:6]
    tabs_f, tabs_b = (refs[6:9], refs[9:12]) if rope else (None, None)
    lg_ref, s0_ref, of_ref, ob_ref, sout_ref, s_ref, mask_ref, dec_ref = refs[12 if rope else 6:]
    t = pl.program_id(2)
    c = qf_ref.shape[1]
    dk, dv = RET_QK_DIM, RET_V_DIM
    scale = dk ** -0.5

    @pl.when(t == 0)
    def _():
        rr = lax.broadcasted_iota(jnp.int32, (c, c), 0)
        cc = lax.broadcasted_iota(jnp.int32, (c, c), 1)
        diff = (rr - cc).astype(F32)
        pos = lax.broadcasted_iota(jnp.int32, (c, LANES), 0).astype(F32)
        for h in range(hp):
            lgf, lgb = lg_ref[h, 0:1, :], lg_ref[h, 1:2, :]
            s_ref[2 * h] = s0_ref[0, h, 0]
            s_ref[2 * h + 1] = s0_ref[0, h, 1]
            mask_ref[h] = scale * jnp.where(diff > 0, jnp.exp(diff * lgf[:, :c]),
                                            jnp.where(diff < 0, jnp.exp(-diff * lgb[:, :c]), 2.0))
            lgf1, lgb1 = lgf[:, :LANES], lgb[:, :LANES]
            dec_ref[4 * h] = scale * jnp.exp((pos + 1.0) * lgf1)
            dec_ref[4 * h + 1] = jnp.exp((c - 1.0 - pos) * lgf1)
            dec_ref[4 * h + 2] = scale * jnp.exp((c - pos) * lgb1)
            dec_ref[4 * h + 3] = jnp.exp(pos * lgb1)

    def load(ref, h, tabs):
        x = ref[0, :, h * dk:(h + 1) * dk].astype(F32)
        return _rotary(x, tabs) if rope else x

    heads = []
    for h in range(hp):
        q, k = load(qf_ref, h, tabs_f), load(kf_ref, h, tabs_f)
        qb, kb = load(qb_ref, h, tabs_b), load(kb_ref, h, tabs_b)
        v, vb = vf_ref[0, :, h * dv:(h + 1) * dv], vb_ref[0, :, h * dv:(h + 1) * dv]
        sf, sb = s_ref[2 * h], s_ref[2 * h + 1]
        heads.append((v, sf, sb,
                      _dot_nt(q, k),
                      _dot(q * dec_ref[4 * h], sf),
                      _dot((k * dec_ref[4 * h + 1]).T, v),
                      _dot(qb * dec_ref[4 * h + 2], sb),
                      _dot((kb * dec_ref[4 * h + 3]).T, vb)))
    for h, (v, sf, sb, sc, inter_f, kv_f, inter_b, kv_b) in enumerate(heads):
        sl = slice(h * dv, (h + 1) * dv)
        of_ref[0, :, sl] = (_dot(sc * mask_ref[h], v) + inter_f).astype(of_ref.dtype)
        ob_ref[0, :, sl] = inter_b.astype(ob_ref.dtype)
        s_ref[2 * h] = sf * jnp.exp(c * lg_ref[h, 0:1, :]) + kv_f
        s_ref[2 * h + 1] = sb * jnp.exp(c * lg_ref[h, 1:2, :]) + kv_b

    @pl.when(t == pl.num_programs(2) - 1)
    def _():
        for h in range(hp):
            sout_ref[0, h, 0] = s_ref[2 * h]
            sout_ref[0, h, 1] = s_ref[2 * h + 1]


def _ret_core(pre, tables, lg, s0, hp=RET_HEADS_PER_STEP):
    b, l, _ = pre.shape
    c = min(RET_CHUNK, l)
    nt = l // c
    h_, dk, dv = RET_HEADS, RET_QK_DIM, RET_V_DIM
    ng = h_ // hp
    fwd = lambda off: (lambda bi, g, t: (bi, t, off + g))
    bwd = lambda off: (lambda bi, g, t: (bi, nt - 1 - t, off + g))

    def specs(mk):
        return [pl.BlockSpec((1, c, hp * dk), mk(0)), pl.BlockSpec((1, c, hp * dk), mk(ng)),
                pl.BlockSpec((1, c, hp * dv), mk(ng))]

    rope = tables is not None
    tab_specs, tab_args = [], []
    if rope:
        tab_specs = ([pl.BlockSpec((c, LANES), lambda bi, g, t: (t, 0))] * 3
                     + [pl.BlockSpec((c, LANES), lambda bi, g, t: (nt - 1 - t, 0))] * 3)
        tab_args = list(tables) * 2
    st = pl.BlockSpec((1, hp, 2, dk, dv), lambda bi, g, t: (bi, g, 0, 0, 0))
    return pl.pallas_call(
        functools.partial(_ret_core_kernel, hp=hp, rope=rope),
        grid=(b, ng, nt),
        in_specs=(specs(fwd) + specs(bwd) + tab_specs
                  + [pl.BlockSpec((hp, 2, dv), lambda bi, g, t: (g, 0, 0)), st]),
        out_specs=[pl.BlockSpec((1, c, hp * dv), fwd(0)), pl.BlockSpec((1, c, hp * dv), bwd(0)), st],
        out_shape=[jax.ShapeDtypeStruct((b, l, h_ * dv), BF16),
                   jax.ShapeDtypeStruct((b, l, h_ * dv), BF16),
                   jax.ShapeDtypeStruct((b, h_, 2, dk, dv), F32)],
        scratch_shapes=[pltpu.VMEM((2 * hp, dk, dv), F32), pltpu.VMEM((hp, c, c), F32),
                        pltpu.VMEM((4 * hp, c, LANES), F32)],
        compiler_params=_cparams("parallel", "parallel", "arbitrary"),
        name="retention_core",
    )(pre, pre, pre, pre, pre, pre, *tab_args, lg, s0)


def _attention_layer(x, xc, m, mc, norm_g, w_in, q_g, k_g, w_out, tables, want_ctx):
    hd = ATT_HEADS * ATT_HEAD_DIM
    kvd = ATT_KV_HEADS * ATT_HEAD_DIM
    w_in = w_in.astype(BF16)
    w_out = w_out.astype(BF16)
    gains = jnp.concatenate([jnp.tile(q_g * (ATT_HEAD_DIM ** -0.5 * LOG2E), (ATT_HEADS, 1)),
                             jnp.tile(k_g, (ATT_KV_HEADS, 1))], axis=0)

    def project(t, mm, tabs):
        qkv = _proj(t, norm_g, mm[1], mm[0], w_in, BF16, tm=1024, tn=hd + 2 * kvd)
        qk = _head_prep(qkv, gains, tabs, norm=True)
        return qk, jnp.concatenate([qk[:, :, hd:], qkv[:, :, hd + kvd:]], axis=-1)

    qk, kv = project(x, m, tables)
    qk_c, kv_c = project(xc, mc, None)
    o = _attention(qk, jnp.concatenate([kv, kv_c], axis=1))
    tm = min(512, x.shape[1])
    x = _out_res(_plain_prologue, [o], [_row_spec(tm, hd)], w_out, x, m[2], tm)
    if want_ctx:
        oc = _attention(qk_c, kv_c)
        tmc = min(512, xc.shape[1])
        xc = _out_res(_plain_prologue, [oc], [_row_spec(tmc, hd)], w_out, xc, mc[2], tmc)
    return x, xc


def _gdn_layer(x, xc, m, mc, norm_g, w_in, conv_w, a_log, dt_bias, gn_g, w_out, want_ctx):
    hq, hv, dh = GDN_QK_HEADS, GDN_V_HEADS, GDN_HEAD_DIM
    cw = (2 * hq + hv) * dh
    zw = hv * dh
    w_qkv = w_in[:, :cw].astype(BF16)
    w_z = w_in[:, cw:cw + zw].astype(BF16)
    w_ab = jnp.pad(w_in[:, cw + zw:], ((0, 0), (0, LANES - 4 * hv))).astype(BF16)
    w_out = w_out.astype(BF16)
    pad = (0, LANES - 2 * hv)
    alog_row = jnp.pad(a_log.reshape(-1), pad).reshape(1, LANES)
    dtb_row = jnp.pad(dt_bias.reshape(-1), pad).reshape(1, LANES)

    def project(t, mm):
        return _gdn_in(t, norm_g, mm[1], mm[0], w_qkv, w_z, w_ab, conv_w, n_norm=2 * hq)

    b = x.shape[0]
    qkv_c, z_c, ab_c = project(xc, mc)
    qkv, z, ab = project(x, m)
    s0 = jnp.zeros((b, 2, hv, dh, dh), F32)
    ocf, ocb, sc = _gdn_core(qkv_c, ab_c, alog_row, dtb_row, s0)
    of, ob, _ = _gdn_core(qkv, ab, alog_row, dtb_row, sc)

    def out(of_, ob_, z_, t, mm):
        tm = min(512, t.shape[1])
        gspec = pl.BlockSpec((1, LANES), lambda bi, i: (0, 0))
        return _out_res(_gdn_prologue, [of_, ob_, z_, gn_g.reshape(1, dh)],
                        [_row_spec(tm, zw)] * 3 + [gspec], w_out, t, mm[2], tm)

    x = out(of, ob, z, x, m)
    if want_ctx:
        xc = out(ocf, ocb, z_c, xc, mc)
    return x, xc


def _retention_layer(x, xc, m, mc, norm_g, w_in, decay, w_out, tables, want_ctx):
    h_, dk, dv = RET_HEADS, RET_QK_DIM, RET_V_DIM
    w_in = w_in.astype(BF16)
    w_out = w_out.astype(BF16)
    lg = jnp.broadcast_to(-decay.astype(F32).T[:, :, None], (h_, 2, dv))

    def project(t, mm):
        return _proj(t, norm_g, mm[1], mm[0], w_in, BF16, tm=1024, tn=2048)

    b = x.shape[0]
    vg_c = project(xc, mc)
    vg = project(x, m)
    s0 = jnp.zeros((b, h_, 2, dk, dv), F32)
    ocf, ocb, sc = _ret_core(vg_c, None, lg, s0)
    of, ob, _ = _ret_core(vg, tables, lg, sc)
    vw = h_ * dv

    def out(of_, ob_, vg_, t, mm):
        tm = min(512, t.shape[1])
        return _out_res(_ret_prologue, [of_, ob_, vg_],
                        [_row_spec(tm, vw), _row_spec(tm, vw), _row_spec(tm, vw, col=2)],
                        w_out, t, mm[2], tm)

    x = out(of, ob, vg, x, m)
    if want_ctx:
        xc = out(ocf, ocb, vg_c, xc, mc)
    return x, xc


def _ffn(x, m, norm_g, w_in, conv_w, conv_b, w_out, final_g=None):
    return _convglu(x, norm_g, m[4], m[3], w_in.astype(BF16), conv_w, conv_b, w_out.astype(BF16),
                    m[5], final_g)


def kernel(x, c, ctx, c_ctx, mod_w, mod_b, norm1_g, norm2_g, att_w_in, att_q_g, att_k_g, att_w_out, gdn_w_in, gdn_conv_w, gdn_A_log, gdn_dt_bias, gdn_norm_g, gdn_w_out, ret_w_in, ret_decay, ret_w_out, ffn_w_in, ffn_conv_w, ffn_conv_b, ffn_w_out, final_g):
    b, n, d = x.shape
    depth = mod_w.shape[0]
    cc = jnp.concatenate([c, c_ctx[None, :], jnp.zeros((8 - b - 1, d), F32)], axis=0)
    mods = _modulation(cc, mod_w, mod_b)
    tab_att = _rope_tables(n, ATT_HEAD_DIM)
    tab_ret = _rope_tables(n, RET_QK_DIM)
    xc = ctx
    for i in range(depth):
        last = i == depth - 1
        mi = mods[i].reshape(8, 6, d)
        m = [mi[:b, s][:, None, :] for s in range(6)]
        mc = [jnp.broadcast_to(mi[b, s][None, None, :], (b, 1, d)) for s in range(6)]
        kind, j = i % N_MIXERS, i // N_MIXERS
        if kind == 0:
            x, xc = _attention_layer(x, xc, m, mc, norm1_g[i], att_w_in[j], att_q_g[j], att_k_g[j],
                                     att_w_out[j], tab_att, not last)
        elif kind == 1:
            x, xc = _gdn_layer(x, xc, m, mc, norm1_g[i], gdn_w_in[j], gdn_conv_w[j], gdn_A_log[j],
                               gdn_dt_bias[j], gdn_norm_g[j], gdn_w_out[j], not last)
        else:
            x, xc = _retention_layer(x, xc, m, mc, norm1_g[i], ret_w_in[j], ret_decay[j],
                                     ret_w_out[j], tab_ret, not last)
        x = _ffn(x, m, norm2_g[i], ffn_w_in[i], ffn_conv_w[i], ffn_conv_b[i], ffn_w_out[i],
                 final_g if last else None)
        if not last:
            xc = _ffn(xc, mc, norm2_g[i], ffn_w_in[i], ffn_conv_w[i], ffn_conv_b[i], ffn_w_out[i])
    return x
```

```python
import functools

import jax
import jax.numpy as jnp
from jax import lax
from jax.experimental import pallas as pl
from jax.experimental.pallas import tpu as pltpu

F32 = jnp.float32
BF16 = jnp.bfloat16

EPS = 1e-6
LOG2E = 1.4426950408889634
ROPE_THETA = 10000.0
GRID_W = 64
N_MIXERS = 3

ATT_HEADS, ATT_KV_HEADS, ATT_HEAD_DIM = 8, 2, 128
GDN_QK_HEADS, GDN_V_HEADS, GDN_HEAD_DIM, GDN_CONV_W = 8, 16, 128, 5
RET_HEADS, RET_QK_DIM, RET_V_DIM = 8, 128, 256
FFN_CONV_W = 3

LANES = 128
BF16_ROWS = 16
MXU_DIM = 256
GDN_CHUNK = 128
GDN_INV_BLOCK = 16
GDN_HEADS_PER_STEP = 8
RET_CHUNK = 256
RET_HEADS_PER_STEP = 8
ATT_MAX_UNROLL = 11
VMEM_LIMIT = 56 * 1024 * 1024


def _cparams(*sem):
    return pltpu.CompilerParams(dimension_semantics=sem, vmem_limit_bytes=VMEM_LIMIT)


def _sigmoid(x):
    return 1.0 / (1.0 + jnp.exp(-x))


def _silu(x):
    return x * _sigmoid(x)


def _dot(a, b):
    return jnp.dot(a.astype(BF16), b.astype(BF16), preferred_element_type=F32)


def _dot_nt(a, b):
    return lax.dot_general(a.astype(BF16), b.astype(BF16), (((1,), (1,)), ((), ())),
                           preferred_element_type=F32)


def _mod_kernel(cc_ref, w_ref, b_ref, o_ref):
    s = _silu(cc_ref[...])
    o_ref[0] = _dot(s, w_ref[0]) + b_ref[0]


def _modulation(cc, mod_w, mod_b):
    depth, d, n6 = mod_w.shape
    tn = n6 // 4
    return pl.pallas_call(
        _mod_kernel,
        grid=(depth, n6 // tn),
        in_specs=[pl.BlockSpec((8, d), lambda l, j: (0, 0)),
                  pl.BlockSpec((1, d, tn), lambda l, j: (l, 0, j)),
                  pl.BlockSpec((1, 1, tn), lambda l, j: (l, 0, j))],
        out_specs=pl.BlockSpec((1, 8, tn), lambda l, j: (l, 0, j)),
        out_shape=jax.ShapeDtypeStruct((depth, 8, n6), F32),
        compiler_params=_cparams("parallel", "parallel"),
        name="modulation",
    )(cc, mod_w, mod_b.reshape(depth, 1, n6))


def _norm_mod(x, g, sc, sh):
    ms = jnp.mean(x * x, axis=-1, keepdims=True)
    return (x * lax.rsqrt(ms + EPS)) * g * (1.0 + sc) + sh


def _proj_kernel(x_ref, g_ref, sc_ref, sh_ref, w_ref, o_ref, h_ref):
    @pl.when(pl.program_id(2) == 0)
    def _():
        h_ref[...] = _norm_mod(x_ref[0], g_ref[...], sc_ref[0], sh_ref[0]).astype(BF16)

    o_ref[0] = jnp.dot(h_ref[...], w_ref[...], preferred_element_type=F32).astype(o_ref.dtype)


def _proj(x, g, sc, sh, w, out_dtype, tm, tn):
    b, l, d = x.shape
    n = w.shape[1]
    tm = min(tm, l)
    tn = min(tn, n)
    return pl.pallas_call(
        _proj_kernel,
        grid=(b, l // tm, n // tn),
        in_specs=[pl.BlockSpec((1, tm, d), lambda bi, i, j: (bi, i, 0)),
                  pl.BlockSpec((1, d), lambda bi, i, j: (0, 0)),
                  pl.BlockSpec((1, 1, d), lambda bi, i, j: (bi, 0, 0)),
                  pl.BlockSpec((1, 1, d), lambda bi, i, j: (bi, 0, 0)),
                  pl.BlockSpec((d, tn), lambda bi, i, j: (0, j))],
        out_specs=pl.BlockSpec((1, tm, tn), lambda bi, i, j: (bi, i, j)),
        out_shape=jax.ShapeDtypeStruct((b, l, n), out_dtype),
        scratch_shapes=[pltpu.VMEM((tm, d), BF16)],
        compiler_params=_cparams("parallel", "parallel", "arbitrary"),
        name="norm_mod_proj",
    )(x, g.reshape(1, d), sc, sh, w)


def _rope_tables(n, head_dim):
    rows = n // GRID_W
    row = jnp.repeat(jnp.arange(rows, dtype=F32), GRID_W)
    col = jnp.tile(jnp.arange(GRID_W, dtype=F32), rows)
    axis_dim = head_dim // 2
    inv_freq = ROPE_THETA ** (-jnp.arange(0, axis_dim, 2, dtype=F32) / axis_dim)
    ar, ac = row[:, None] * inv_freq, col[:, None] * inv_freq
    cr, sr, cc, sc = jnp.cos(ar), jnp.sin(ar), jnp.cos(ac), jnp.sin(ac)
    z = jnp.zeros_like(sr)
    return (jnp.concatenate([cr, cr, cc, cc], axis=-1),
            jnp.concatenate([-sr, z, -sc, z], axis=-1),
            jnp.concatenate([z, sr, z, sc], axis=-1))


def _head_prep_kernel(*refs, norm, rope):
    if rope:
        x_ref, g_ref, c_ref, sa_ref, sb_ref, o_ref = refs
    else:
        x_ref, g_ref, o_ref = refs
    if rope:
        q4 = LANES // 4
        src = lax.broadcasted_iota(jnp.int32, (LANES, LANES), 0)
        dst = lax.broadcasted_iota(jnp.int32, (LANES, LANES), 1)
        low = (dst % (2 * q4)) < q4
        perm = jnp.where(jnp.logical_and(low, src == dst + q4), -1.0,
                         jnp.where(jnp.logical_and(jnp.logical_not(low), src == dst - q4),
                                   1.0, 0.0)).astype(BF16)
        sin = sb_ref[...] - sa_ref[...]
    for h in range(g_ref.shape[1]):
        sl = slice(h * LANES, (h + 1) * LANES)
        x = x_ref[0, :, sl].astype(F32)
        if norm:
            x = x * lax.rsqrt(jnp.mean(x * x, axis=-1, keepdims=True) + EPS)
        x = x * g_ref[0, h:h + 1, :]
        if rope:
            x = x * c_ref[...] + jnp.dot(x.astype(BF16), perm, preferred_element_type=F32) * sin
        o_ref[0, :, sl] = x.astype(o_ref.dtype)


def _head_prep(x, gains, tables, norm, tm=512):
    b, l, _ = x.shape
    nb = gains.shape[0]
    tm = min(tm, l)
    hw = max(h for h in (8, 5, 4, 2, 1) if nb % h == 0)
    in_specs = [pl.BlockSpec((1, tm, hw * LANES), lambda bi, i, c: (bi, i, c)),
                pl.BlockSpec((1, hw, LANES), lambda bi, i, c: (c, 0, 0))]
    args = [x, gains.reshape(nb // hw, hw, LANES)]
    if tables is not None:
        in_specs += [pl.BlockSpec((tm, LANES), lambda bi, i, c: (i, 0))] * 3
        args += list(tables)
    return pl.pallas_call(
        functools.partial(_head_prep_kernel, norm=norm, rope=tables is not None),
        grid=(b, l // tm, nb // hw),
        in_specs=in_specs,
        out_specs=pl.BlockSpec((1, tm, hw * LANES), lambda bi, i, c: (bi, i, c)),
        out_shape=jax.ShapeDtypeStruct((b, l, nb * LANES), BF16),
        compiler_params=_cparams("parallel", "parallel", "parallel"),
        name="head_prep",
    )(*args)


def _attn_kernel(q_ref, k_ref, vt_ref, o_ref, *, groups, tq, unroll):
    dh = LANES
    qt = jnp.concatenate([q_ref[0, :, g * dh:(g + 1) * dh].astype(F32).T for g in range(groups)],
                         axis=1).astype(BF16)
    rows = groups * tq
    nk, vrows, tk = vt_ref.shape[2:]

    def scores(j):
        off = pl.multiple_of(j * tk, tk)
        return jnp.dot(k_ref[0, pl.ds(off, tk), :], qt, preferred_element_type=F32)

    def body(it, carry):
        m, acc = carry
        base = it * unroll
        st = scores(base)
        pv = alpha = None
        for u in range(unroll):
            st_next = scores(base + u + 1) if u + 1 < unroll else None
            m_new = jnp.maximum(m, jnp.max(st, axis=0, keepdims=True))
            p = jnp.exp2(st - m_new).astype(BF16)
            if pv is not None:
                acc = alpha * acc + pv
            alpha = jnp.exp2(m - m_new)
            pv = jnp.dot(vt_ref[0, 0, base + u], p, preferred_element_type=F32)
            m, st = m_new, st_next
        return m, alpha * acc + pv

    init = (jnp.full((1, rows), -1e30, F32), jnp.zeros((vrows, rows), F32))
    _, acc = lax.fori_loop(0, nk // unroll, body, init)
    out = acc[:dh] / acc[dh:dh + 1]
    for g in range(groups):
        o_ref[0, :, g * dh:(g + 1) * dh] = out[:, g * tq:(g + 1) * tq].T.astype(o_ref.dtype)


def _attention(q, kv, tq=1024, tk=256):
    b, n, _ = q.shape
    lk = kv.shape[1]
    kvh = ATT_KV_HEADS
    groups = ATT_HEADS // kvh
    hd = ATT_HEADS * LANES
    gw = groups * LANES
    tq = min(tq, n)
    tk = min(tk, lk)
    nk = lk // tk
    vt = kv[:, :, kvh * LANES:].reshape(b, nk, tk, kvh, LANES).transpose(0, 3, 1, 4, 2)
    vt = jnp.concatenate([vt, jnp.ones((b, kvh, nk, BF16_ROWS, tk), BF16)], axis=3)
    vrows = LANES + BF16_ROWS
    unroll = max(u for u in range(1, ATT_MAX_UNROLL + 1) if nk % u == 0)
    return pl.pallas_call(
        functools.partial(_attn_kernel, groups=groups, tq=tq, unroll=unroll),
        grid=(b, kvh, n // tq),
        in_specs=[pl.BlockSpec((1, tq, gw), lambda bi, h, i: (bi, i, h)),
                  pl.BlockSpec((1, lk, LANES), lambda bi, h, i: (bi, 0, h)),
                  pl.BlockSpec((1, 1, nk, vrows, tk), lambda bi, h, i: (bi, h, 0, 0, 0))],
        out_specs=pl.BlockSpec((1, tq, gw), lambda bi, h, i: (bi, i, h)),
        out_shape=jax.ShapeDtypeStruct((b, n, hd), BF16),
        compiler_params=_cparams("parallel", "parallel", "parallel"),
        name="gqa_attention",
    )(q, kv, vt)


def _plain_prologue(j, a_ref):
    return a_ref[0, :, j * MXU_DIM:(j + 1) * MXU_DIM]


def _gdn_prologue(j, of_ref, ob_ref, z_ref, g_ref):
    outs = []
    g = g_ref[...]
    for h in range(j * MXU_DIM // LANES, (j + 1) * MXU_DIM // LANES):
        sl = slice(h * LANES, (h + 1) * LANES)
        o = of_ref[0, :, sl].astype(F32) + ob_ref[0, :, sl].astype(F32)
        o = o * lax.rsqrt(jnp.mean(o * o, axis=-1, keepdims=True) + EPS) * g
        outs.append((o * _silu(z_ref[0, :, sl].astype(F32))).astype(BF16))
    return jnp.concatenate(outs, axis=-1)


def _ret_prologue(j, of_ref, ob_ref, gate_ref):
    sl = slice(j * RET_V_DIM, (j + 1) * RET_V_DIM)
    o = of_ref[0, :, sl].astype(F32) + ob_ref[0, :, sl].astype(F32)
    o = o * lax.rsqrt(jnp.mean(o * o, axis=-1, keepdims=True) + EPS)
    return (o * _silu(gate_ref[0, :, sl].astype(F32))).astype(BF16)


def _out_res_kernel(*refs, prologue, n_in):
    w_ref, x_ref, ga_ref, o_ref = refs[n_in:]
    ins = refs[:n_in]
    n = w_ref.shape[0] // MXU_DIM
    a, y = prologue(0, *ins), None
    for j in range(n):
        a_next = prologue(j + 1, *ins) if j + 1 < n else None
        part = jnp.dot(a, w_ref[j * MXU_DIM:(j + 1) * MXU_DIM, :], preferred_element_type=F32)
        y = part if y is None else y + part
        a = a_next
    o_ref[0] = x_ref[0] + ga_ref[0] * y


def _out_res(prologue, ins, in_specs, w, x, ga, tm):
    b, l, d = x.shape
    k = w.shape[0]
    return pl.pallas_call(
        functools.partial(_out_res_kernel, prologue=prologue, n_in=len(ins)),
        grid=(b, l // tm),
        in_specs=in_specs + [pl.BlockSpec((k, d), lambda bi, i: (0, 0)),
                             pl.BlockSpec((1, tm, d), lambda bi, i: (bi, i, 0)),
                             pl.BlockSpec((1, 1, d), lambda bi, i: (bi, 0, 0))],
        out_specs=pl.BlockSpec((1, tm, d), lambda bi, i: (bi, i, 0)),
        out_shape=jax.ShapeDtypeStruct((b, l, d), F32),
        compiler_params=_cparams("parallel", "parallel"),
        name="out_proj_residual",
    )(*ins, w, x, ga)


def _row_spec(tm, width, col=0):
    return pl.BlockSpec((1, tm, width), lambda bi, i: (bi, i, col))


def _convglu_kernel(x_ref, xp_ref, xn_ref, g_ref, sc_ref, sh_ref, win_ref, cw_ref, cb_ref, wout_ref,
                    ga_ref, fg_ref, o_ref, *, sub, final):
    i = pl.program_id(1)
    tm = x_ref.shape[1]
    f = wout_ref.shape[0]
    hl = xp_ref.shape[1]
    x = x_ref[0]
    xe = jnp.concatenate([xp_ref[0], x, xn_ref[0]], axis=0)
    he = _norm_mod(xe, g_ref[...], sc_ref[0], sh_ref[0]).astype(BF16)
    h = he[hl:hl + tm]
    r = lax.broadcasted_iota(jnp.int32, (tm + 2 * hl, 1), 0)
    inside = jnp.logical_and(jnp.logical_or(r >= hl, i > 0),
                             jnp.logical_or(r < hl + tm, i < pl.num_programs(1) - 1))
    keep = jnp.where(inside, 1.0, 0.0)

    def project(j):
        gate = jnp.dot(he, win_ref[:, f + j * sub:f + (j + 1) * sub],
                       preferred_element_type=F32) * keep
        val = jnp.dot(h, win_ref[:, j * sub:(j + 1) * sub], preferred_element_type=F32)
        return gate, val

    def act(gate, val, j):
        sl = slice(j * sub, (j + 1) * sub)
        cw = cw_ref[:, sl]
        conv = (cw[0:1] * gate[hl - 1:hl - 1 + tm] + cw[1:2] * gate[hl:hl + tm]
                + cw[2:3] * gate[hl + 1:hl + 1 + tm] + cb_ref[:, sl])
        gelu = 0.5 * conv * (1.0 + lax.erf(conv * (2.0 ** -0.5)))
        return (gelu * val).astype(BF16)

    n = f // sub
    gv, y = project(0), None
    for j in range(n):
        gv_next = project(j + 1) if j + 1 < n else None
        a = act(*gv, j)
        part = jnp.dot(a, wout_ref[j * sub:(j + 1) * sub, :], preferred_element_type=F32)
        y = part if y is None else y + part
        gv = gv_next
    out = x + ga_ref[0] * y
    if final:
        out = out * lax.rsqrt(jnp.mean(out * out, axis=-1, keepdims=True) + EPS) * fg_ref[...]
    o_ref[0] = out


def _convglu(x, g, sc, sh, w_in, conv_w, conv_b, w_out, ga, final_g=None, tm=512):
    b, l, d = x.shape
    f = w_out.shape[0]
    tm = min(tm, l)
    sub = MXU_DIM if f % MXU_DIM == 0 else LANES
    hl = BF16_ROWS
    hb = tm // hl
    last_hb = l // hl - 1
    final = final_g is not None
    fg = (final_g if final else jnp.ones((d,), F32)).reshape(1, d)
    const = lambda shape: pl.BlockSpec(shape, lambda bi, i: (0,) * len(shape))
    mod = pl.BlockSpec((1, 1, d), lambda bi, i: (bi, 0, 0))
    return pl.pallas_call(
        functools.partial(_convglu_kernel, sub=sub, final=final),
        grid=(b, l // tm),
        in_specs=[pl.BlockSpec((1, tm, d), lambda bi, i: (bi, i, 0)),
                  pl.BlockSpec((1, hl, d), lambda bi, i: (bi, jnp.maximum(i * hb - 1, 0), 0)),
                  pl.BlockSpec((1, hl, d), lambda bi, i: (bi, jnp.minimum((i + 1) * hb, last_hb), 0)),
                  const((1, d)), mod, mod, const((d, 2 * f)), const((FFN_CONV_W, f)),
                  const((1, f)), const((f, d)), mod, const((1, d))],
        out_specs=pl.BlockSpec((1, tm, d), lambda bi, i: (bi, i, 0)),
        out_shape=jax.ShapeDtypeStruct((b, l, d), F32),
        compiler_params=_cparams("parallel", "parallel"),
        name="convglu",
    )(x, x, x, g.reshape(1, d), sc, sh, w_in, conv_w, conv_b.reshape(1, f), w_out, ga, fg)


def _gdn_in_kernel(x_ref, xp_ref, xn_ref, g_ref, sc_ref, sh_ref, w_ref, wz_ref, wab_ref, cw_ref,
                   qkv_ref, z_ref, ab_ref, *, n_norm, sub):
    i = pl.program_id(1)
    tm = x_ref.shape[1]
    hl = xp_ref.shape[1]
    ch = cw_ref.shape[1]
    pad = GDN_CONV_W // 2
    xe = jnp.concatenate([xp_ref[0], x_ref[0], xn_ref[0]], axis=0)
    he = _norm_mod(xe, g_ref[...], sc_ref[0], sh_ref[0]).astype(BF16)
    h = he[hl:hl + tm]
    r = lax.broadcasted_iota(jnp.int32, (tm + 2 * hl, 1), 0)
    inside = jnp.logical_and(jnp.logical_or(r >= hl, i > 0),
                             jnp.logical_or(r < hl + tm, i < pl.num_programs(1) - 1))
    keep = jnp.where(inside, 1.0, 0.0)

    def project(j):
        return jnp.dot(he, w_ref[:, j * sub:(j + 1) * sub], preferred_element_type=F32) * keep

    def act(pre, j):
        for hh in range(sub // LANES):
            lo = j * sub + hh * LANES
            cw = cw_ref[:, lo:lo + LANES]
            src = pre[:, hh * LANES:(hh + 1) * LANES]
            y = None
            for t in range(GDN_CONV_W):
                tap = src if t == pad else pltpu.roll(src, (pad - t) % src.shape[0], 0)
                term = cw[t:t + 1] * tap[hl:hl + tm]
                y = term if y is None else y + term
            y = _silu(y)
            if lo // LANES < n_norm:
                y = y * lax.rsqrt(jnp.sum(y * y, axis=-1, keepdims=True) + EPS)
            qkv_ref[0, :, lo:lo + LANES] = y.astype(qkv_ref.dtype)

    n = ch // sub
    pre = project(0)
    for j in range(n):
        pre_next = project(j + 1) if j + 1 < n else None
        if j == n - 1:
            z_ref[0] = jnp.dot(h, wz_ref[...], preferred_element_type=F32).astype(z_ref.dtype)
            ab_ref[0] = jnp.dot(h, wab_ref[...], preferred_element_type=F32)
        act(pre, j)
        pre = pre_next


def _gdn_in(x, g, sc, sh, w_qkv, w_z, w_ab, conv_w, n_norm, tm=512):
    b, l, d = x.shape
    ch, zw = w_qkv.shape[1], w_z.shape[1]
    tm = min(tm, l)
    hl = BF16_ROWS
    hb = tm // hl
    last_hb = l // hl - 1
    const = lambda shape: pl.BlockSpec(shape, lambda bi, i: (0,) * len(shape))
    mod = pl.BlockSpec((1, 1, d), lambda bi, i: (bi, 0, 0))
    rows = lambda w: pl.BlockSpec((1, tm, w), lambda bi, i: (bi, i, 0))
    return pl.pallas_call(
        functools.partial(_gdn_in_kernel, n_norm=n_norm, sub=MXU_DIM),
        grid=(b, l // tm),
        in_specs=[rows(d),
                  pl.BlockSpec((1, hl, d), lambda bi, i: (bi, jnp.maximum(i * hb - 1, 0), 0)),
                  pl.BlockSpec((1, hl, d), lambda bi, i: (bi, jnp.minimum((i + 1) * hb, last_hb), 0)),
                  const((1, d)), mod, mod, const((d, ch)), const((d, zw)), const((d, LANES)),
                  const((GDN_CONV_W, ch))],
        out_specs=[rows(ch), rows(zw), rows(LANES)],
        out_shape=[jax.ShapeDtypeStruct((b, l, ch), BF16), jax.ShapeDtypeStruct((b, l, zw), BF16),
                   jax.ShapeDtypeStruct((b, l, LANES), F32)],
        compiler_params=_cparams("parallel", "parallel"),
        name="gdn_in",
    )(x, x, x, g.reshape(1, d), sc, sh, w_qkv, w_z, w_ab, conv_w)


def _mm(xs, ys):
    return [jnp.dot(x, y, preferred_element_type=F32) for x, y in zip(xs, ys)]


def _bf(xs):
    return [x.astype(BF16) for x in xs]


def _neumann_minus_eye(ms, order):
    mb = _bf(ms)
    m2 = _mm(mb, mb)
    pb = _bf(m2)
    m3 = _mm(mb, pb)
    r = [b - a - c for a, b, c in zip(ms, m2, m3)]
    p, mp = 4, m2
    while p < order:
        mp = _mm(pb, pb)
        pb = _bf(mp)
        rm = _mm(_bf(r), pb)
        r = [x + y + z for x, y, z in zip(r, mp, rm)]
        p *= 2
    return r


def _unit_tri_inverse_minus_eye(a_list, same_block):
    c = a_list[0].shape[0]
    d = [jnp.where(same_block, a, 0.0) for a in a_list]
    lo = [a - x for a, x in zip(a_list, d)]
    r0 = _neumann_minus_eye(d, GDN_INV_BLOCK)
    r0b = _bf(r0)
    n = [x + y for x, y in zip(lo, _mm(r0b, _bf(lo)))]
    rn = _neumann_minus_eye(n, c // GDN_INV_BLOCK)
    return [x + y + z for x, y, z in zip(r0, rn, _mm(_bf(rn), r0b))]


def _gdn_core_kernel(qf_ref, kf_ref, vf_ref, abf_ref, qb_ref, kb_ref, vb_ref, abb_ref,
                     alog_ref, dtb_ref, s0_ref, of_ref, ob_ref, sout_ref, s_ref, gt_ref, *, hb):
    t = pl.program_id(2)
    grp = pl.program_id(1)
    rep = GDN_V_HEADS // GDN_QK_HEADS
    c, dk = GDN_CHUNK, GDN_HEAD_DIM
    scale = dk ** -0.5

    @pl.when(t == 0)
    def _():
        s_ref[...] = s0_ref[0]

    rr = lax.broadcasted_iota(jnp.int32, (c, c), 0)
    cc = lax.broadcasted_iota(jnp.int32, (c, c), 1)
    same_block = (rr // GDN_INV_BLOCK) == (cc // GDN_INV_BLOCK)
    lane = lax.broadcasted_iota(jnp.int32, (c, LANES), 1)
    ng = 2 * GDN_V_HEADS
    dirs = ((qf_ref, kf_ref, vf_ref, abf_ref, of_ref, rr >= cc, rr > cc),
            (qb_ref, kb_ref, vb_ref, abb_ref, ob_ref, rr <= cc, rr < cc))

    gcs, balls, kks, qks, kts = [], [], [], [], []
    for q_ref, k_ref, _, ab_ref, _, incl, _ in dirs:
        ab = ab_ref[0]
        sp = ab + dtb_ref[...]
        sp = jnp.maximum(sp, 0.0) + jnp.log1p(jnp.exp(-jnp.abs(sp)))
        gall = -jnp.exp(alog_ref[...]) * sp
        balls.append(_sigmoid(ab))
        g1 = gall.astype(BF16)
        r1 = gall - g1.astype(F32)
        g2 = r1.astype(BF16)
        g3 = (r1 - g2.astype(F32)).astype(BF16)
        packed = jnp.where(lane < ng, g1.astype(F32),
                           jnp.where(lane < 2 * ng, pltpu.roll(g2.astype(F32), ng, 1),
                                     jnp.where(lane < 3 * ng,
                                               pltpu.roll(g3.astype(F32), 2 * ng, 1), 0.0)))
        sel = jnp.concatenate([jnp.where(incl, 1.0, 0.0), jnp.ones((c, c), F32)], axis=0)
        cs = _dot(sel, packed)
        cs = cs + pltpu.roll(cs, LANES - ng, 1) + pltpu.roll(cs, LANES - 2 * ng, 1)
        gcs.append(cs[:c])
        d = len(gcs) - 1
        gt_ref[d, 0] = cs[:c].T
        gt_ref[d, 1] = cs[c:].T
        for hh in range(hb):
            sl = slice(hh * dk, (hh + 1) * dk)
            k = k_ref[0, :, sl]
            g = _dot_nt(jnp.concatenate([k, q_ref[0, :, sl]], axis=0), k)
            kks.append(g[:c])
            qks.append(g[c:])
            kts.append(k.astype(F32).T)

    chains = [(d, hh, e) for d in range(2) for hh in range(hb) for e in range(rep)]
    a_list, xs, qkms, qds, kdts, gls = [], [], [], [], [], []

    def prepare(d, hh, e):
        q_ref, k_ref, v_ref, _, _, incl, strict = dirs[d]
        col = d * GDN_V_HEADS + (grp * hb + hh) * rep + e
        pick = lambda m, j: jnp.sum(jnp.where(lane == j, m, 0.0), axis=-1, keepdims=True)
        gc = pick(gcs[d], col)
        bcol = pick(balls[d], col + ng)
        gc_row = gt_ref[d, 0, pl.ds(col, 1), :]
        gtot_row = gt_ref[d, 1, pl.ds(col, 1), :]
        decay = jnp.where(incl, jnp.exp(jnp.minimum(gc - gc_row, 0.0)), 0.0)
        a_list.append(jnp.where(strict, bcol * kks[d * hb + hh] * decay, 0.0))
        kf = k_ref[0, :, hh * dk:(hh + 1) * dk].astype(F32)
        qf = q_ref[0, :, hh * dk:(hh + 1) * dk].astype(F32)
        vf = v_ref[0, :, (hh * rep + e) * dk:(hh * rep + e + 1) * dk].astype(F32)
        egc = jnp.exp(gc)
        xs.append(jnp.concatenate([vf * bcol, kf * (bcol * egc)], axis=1))
        qkms.append((qks[d * hb + hh] * decay * scale).astype(BF16))
        qds.append(qf * (egc * scale))
        kdts.append((kts[d * hb + hh] * jnp.exp(gtot_row - gc_row)).astype(BF16))
        gls.append(jnp.exp(gtot_row[:, :dk]))

    for ch in chains:
        prepare(*ch)

    tms = _unit_tri_inverse_minus_eye(a_list, same_block)
    uws = [x + y for x, y in zip(xs, _mm(_bf(tms), _bf(xs)))]

    ss = [s_ref[i] for i in range(len(chains))]
    sbs = _bf(ss)
    wss = _mm([jnp.concatenate([uw[:, dk:], qd], axis=0).astype(BF16) for uw, qd in zip(uws, qds)],
              sbs)
    vns = [(uw[:, :dk] - ws[:c]).astype(BF16) for uw, ws in zip(uws, wss)]
    ovs = _mm(qkms, vns)
    kvs = _mm(kdts, vns)
    for i, (d, hh, e) in enumerate(chains):
        o_ref = dirs[d][4]
        o_ref[0, :, (hh * rep + e) * dk:(hh * rep + e + 1) * dk] = (
            wss[i][c:] + ovs[i]).astype(o_ref.dtype)
        s_ref[i] = ss[i] * gls[i] + kvs[i]

    @pl.when(t == pl.num_programs(2) - 1)
    def _():
        sout_ref[0] = s_ref[...]


def _gdn_core(qkv, ab, alog_row, dtb_row, s0, hb=GDN_HEADS_PER_STEP):
    b, l, _ = qkv.shape
    c = GDN_CHUNK
    nt = l // c
    hq, hv = GDN_QK_HEADS, GDN_V_HEADS
    rep = hv // hq
    ngrp = hq // hb
    fwd = lambda off: (lambda bi, g, t: (bi, t, off + g))
    bwd = lambda off: (lambda bi, g, t: (bi, nt - 1 - t, off + g))
    qw, vw = hb * LANES, hb * rep * LANES

    def specs(mk, ab_map):
        return [pl.BlockSpec((1, c, qw), mk(0)),
                pl.BlockSpec((1, c, qw), mk(ngrp)),
                pl.BlockSpec((1, c, vw), mk(ngrp)),
                pl.BlockSpec((1, c, LANES), ab_map)]

    ab_fwd = lambda bi, g, t: (bi, t, 0)
    ab_bwd = lambda bi, g, t: (bi, nt - 1 - t, 0)
    row = pl.BlockSpec((1, LANES), lambda bi, g, t: (0, 0))
    st = pl.BlockSpec((1, 2 * hb * rep, LANES, LANES), lambda bi, g, t: (bi * ngrp + g, 0, 0, 0))
    s0g = s0.reshape(b, 2, ngrp, hb * rep, LANES, LANES).transpose(0, 2, 1, 3, 4, 5)
    s0g = s0g.reshape(b * ngrp, 2 * hb * rep, LANES, LANES)
    of, ob, sg = pl.pallas_call(
        functools.partial(_gdn_core_kernel, hb=hb),
        grid=(b, ngrp, nt),
        in_specs=specs(fwd, ab_fwd) + specs(bwd, ab_bwd) + [row, row, st],
        out_specs=[pl.BlockSpec((1, c, vw), fwd(0)), pl.BlockSpec((1, c, vw), bwd(0)), st],
        out_shape=[jax.ShapeDtypeStruct((b, l, hv * LANES), BF16),
                   jax.ShapeDtypeStruct((b, l, hv * LANES), BF16),
                   jax.ShapeDtypeStruct(s0g.shape, F32)],
        scratch_shapes=[pltpu.VMEM((2 * hb * rep, LANES, LANES), F32),
                        pltpu.VMEM((2, 2, LANES, c), F32)],
        compiler_params=_cparams("parallel", "parallel", "arbitrary"),
        name="gdn_core",
    )(qkv, qkv, qkv, ab, qkv, qkv, qkv, ab, alog_row, dtb_row, s0g)
    sg = sg.reshape(b, ngrp, 2, hb * rep, LANES, LANES).transpose(0, 2, 1, 3, 4, 5)
    return of, ob, sg.reshape(b, 2, hv, LANES, LANES)


def _rotary(x, tabs):
    c_ref, sa_ref, sb_ref = tabs
    return (x * c_ref[...] + pltpu.roll(x, 96, 1) * sa_ref[...]
            + pltpu.roll(x, 32, 1) * sb_ref[...])


def _ret_core_kernel(*refs, hp, rope):
    qf_ref, kf_ref, vf_ref, qb_ref, kb_ref, vb_ref = refs[:6]
    tabs_f, tabs_b = (refs[6:9], refs[9:12]) if rope else (None, None)
    lg_ref, s0_ref, of_ref, ob_ref, sout_ref, s_ref, mask_ref, dec_ref = refs[12 if rope else 6:]
    t = pl.program_id(2)
    c = qf_ref.shape[1]
    dk, dv = RET_QK_DIM, RET_V_DIM
    scale = dk ** -0.5

    @pl.when(t == 0)
    def _():
        rr = lax.broadcasted_iota(jnp.int32, (c, c), 0)
        cc = lax.broadcasted_iota(jnp.int32, (c, c), 1)
        diff = (rr - cc).astype(F32)
        pos = lax.broadcasted_iota(jnp.int32, (c, LANES), 0).astype(F32)
        for h in range(hp):
            lgf, lgb = lg_ref[h, 0:1, :], lg_ref[h, 1:2, :]
            s_ref[2 * h] = s0_ref[0, h, 0]
            s_ref[2 * h + 1] = s0_ref[0, h, 1]
            mask_ref[h] = scale * jnp.where(diff > 0, jnp.exp(diff * lgf[:, :c]),
                                            jnp.where(diff < 0, jnp.exp(-diff * lgb[:, :c]), 2.0))
            lgf1, lgb1 = lgf[:, :LANES], lgb[:, :LANES]
            dec_ref[4 * h] = scale * jnp.exp((pos + 1.0) * lgf1)
            dec_ref[4 * h + 1] = jnp.exp((c - 1.0 - pos) * lgf1)
            dec_ref[4 * h + 2] = scale * jnp.exp((c - pos) * lgb1)
            dec_ref[4 * h + 3] = jnp.exp(pos * lgb1)

    def load(ref, h, tabs):
        x = ref[0, :, h * dk:(h + 1) * dk].astype(F32)
        return _rotary(x, tabs) if rope else x

    heads = []
    for h in range(hp):
        q, k = load(qf_ref, h, tabs_f), load(kf_ref, h, tabs_f)
        qb, kb = load(qb_ref, h, tabs_b), load(kb_ref, h, tabs_b)
        v, vb = vf_ref[0, :, h * dv:(h + 1) * dv], vb_ref[0, :, h * dv:(h + 1) * dv]
        sf, sb = s_ref[2 * h], s_ref[2 * h + 1]
        heads.append((v, sf, sb,
                      _dot_nt(q, k),
                      _dot(q * dec_ref[4 * h], sf),
                      _dot((k * dec_ref[4 * h + 1]).T, v),
                      _dot(qb * dec_ref[4 * h + 2], sb),
                      _dot((kb * dec_ref[4 * h + 3]).T, vb)))
    for h, (v, sf, sb, sc, inter_f, kv_f, inter_b, kv_b) in enumerate(heads):
        sl = slice(h * dv, (h + 1) * dv)
        of_ref[0, :, sl] = (_dot(sc * mask_ref[h], v) + inter_f).astype(of_ref.dtype)
        ob_ref[0, :, sl] = inter_b.astype(ob_ref.dtype)
        s_ref[2 * h] = sf * jnp.exp(c * lg_ref[h, 0:1, :]) + kv_f
        s_ref[2 * h + 1] = sb * jnp.exp(c * lg_ref[h, 1:2, :]) + kv_b

    @pl.when(t == pl.num_programs(2) - 1)
    def _():
        for h in range(hp):
            sout_ref[0, h, 0] = s_ref[2 * h]
            sout_ref[0, h, 1] = s_ref[2 * h + 1]


def _ret_core(pre, tables, lg, s0, hp=RET_HEADS_PER_STEP):
    b, l, _ = pre.shape
    c = min(RET_CHUNK, l)
    nt = l // c
    h_, dk, dv = RET_HEADS, RET_QK_DIM, RET_V_DIM
    ng = h_ // hp
    fwd = lambda off: (lambda bi, g, t: (bi, t, off + g))
    bwd = lambda off: (lambda bi, g, t: (bi, nt - 1 - t, off + g))

    def specs(mk):
        return [pl.BlockSpec((1, c, hp * dk), mk(0)), pl.BlockSpec((1, c, hp * dk), mk(ng)),
                pl.BlockSpec((1, c, hp * dv), mk(ng))]

    rope = tables is not None
    tab_specs, tab_args = [], []
    if rope:
        tab_specs = ([pl.BlockSpec((c, LANES), lambda bi, g, t: (t, 0))] * 3
                     + [pl.BlockSpec((c, LANES), lambda bi, g, t: (nt - 1 - t, 0))] * 3)
        tab_args = list(tables) * 2
    st = pl.BlockSpec((1, hp, 2, dk, dv), lambda bi, g, t: (bi, g, 0, 0, 0))
    return pl.pallas_call(
        functools.partial(_ret_core_kernel, hp=hp, rope=rope),
        grid=(b, ng, nt),
        in_specs=(specs(fwd) + specs(bwd) + tab_specs
                  + [pl.BlockSpec((hp, 2, dv), lambda bi, g, t: (g, 0, 0)), st]),
        out_specs=[pl.BlockSpec((1, c, hp * dv), fwd(0)), pl.BlockSpec((1, c, hp * dv), bwd(0)), st],
        out_shape=[jax.ShapeDtypeStruct((b, l, h_ * dv), BF16),
                   jax.ShapeDtypeStruct((b, l, h_ * dv), BF16),
                   jax.ShapeDtypeStruct((b, h_, 2, dk, dv), F32)],
        scratch_shapes=[pltpu.VMEM((2 * hp, dk, dv), F32), pltpu.VMEM((hp, c, c), F32),
                        pltpu.VMEM((4 * hp, c, LANES), F32)],
        compiler_params=_cparams("parallel", "parallel", "arbitrary"),
        name="retention_core",
    )(pre, pre, pre, pre, pre, pre, *tab_args, lg, s0)


def _attention_layer(x, xc, m, mc, norm_g, w_in, q_g, k_g, w_out, tables, want_ctx):
    hd = ATT_HEADS * ATT_HEAD_DIM
    kvd = ATT_KV_HEADS * ATT_HEAD_DIM
    w_in = w_in.astype(BF16)
    w_out = w_out.astype(BF16)
    gains = jnp.concatenate([jnp.tile(q_g * (ATT_HEAD_DIM ** -0.5 * LOG2E), (ATT_HEADS, 1)),
                             jnp.tile(k_g, (ATT_KV_HEADS, 1))], axis=0)

    def project(t, mm, tabs):
        qkv = _proj(t, norm_g, mm[1], mm[0], w_in, BF16, tm=1024, tn=hd + 2 * kvd)
        qk = _head_prep(qkv, gains, tabs, norm=True)
        return qk, jnp.concatenate([qk[:, :, hd:], qkv[:, :, hd + kvd:]], axis=-1)

    qk, kv = project(x, m, tables)
    qk_c, kv_c = project(xc, mc, None)
    o = _attention(qk, jnp.concatenate([kv, kv_c], axis=1))
    tm = min(512, x.shape[1])
    x = _out_res(_plain_prologue, [o], [_row_spec(tm, hd)], w_out, x, m[2], tm)
    if want_ctx:
        oc = _attention(qk_c, kv_c)
        tmc = min(512, xc.shape[1])
        xc = _out_res(_plain_prologue, [oc], [_row_spec(tmc, hd)], w_out, xc, mc[2], tmc)
    return x, xc


def _gdn_layer(x, xc, m, mc, norm_g, w_in, conv_w, a_log, dt_bias, gn_g, w_out, want_ctx):
    hq, hv, dh = GDN_QK_HEADS, GDN_V_HEADS, GDN_HEAD_DIM
    cw = (2 * hq + hv) * dh
    zw = hv * dh
    w_qkv = w_in[:, :cw].astype(BF16)
    w_z = w_in[:, cw:cw + zw].astype(BF16)
    w_ab = jnp.pad(w_in[:, cw + zw:], ((0, 0), (0, LANES - 4 * hv))).astype(BF16)
    w_out = w_out.astype(BF16)
    pad = (0, LANES - 2 * hv)
    alog_row = jnp.pad(a_log.reshape(-1), pad).reshape(1, LANES)
    dtb_row = jnp.pad(dt_bias.reshape(-1), pad).reshape(1, LANES)

    def project(t, mm):
        return _gdn_in(t, norm_g, mm[1], mm[0], w_qkv, w_z, w_ab, conv_w, n_norm=2 * hq)

    b = x.shape[0]
    qkv_c, z_c, ab_c = project(xc, mc)
    qkv, z, ab = project(x, m)
    s0 = jnp.zeros((b, 2, hv, dh, dh), F32)
    ocf, ocb, sc = _gdn_core(qkv_c, ab_c, alog_row, dtb_row, s0)
    of, ob, _ = _gdn_core(qkv, ab, alog_row, dtb_row, sc)

    def out(of_, ob_, z_, t, mm):
        tm = min(512, t.shape[1])
        gspec = pl.BlockSpec((1, LANES), lambda bi, i: (0, 0))
        return _out_res(_gdn_prologue, [of_, ob_, z_, gn_g.reshape(1, dh)],
                        [_row_spec(tm, zw)] * 3 + [gspec], w_out, t, mm[2], tm)

    x = out(of, ob, z, x, m)
    if want_ctx:
        xc = out(ocf, ocb, z_c, xc, mc)
    return x, xc


def _retention_layer(x, xc, m, mc, norm_g, w_in, decay, w_out, tables, want_ctx):
    h_, dk, dv = RET_HEADS, RET_QK_DIM, RET_V_DIM
    w_in = w_in.astype(BF16)
    w_out = w_out.astype(BF16)
    lg = jnp.broadcast_to(-decay.astype(F32).T[:, :, None], (h_, 2, dv))

    def project(t, mm):
        return _proj(t, norm_g, mm[1], mm[0], w_in, BF16, tm=1024, tn=2048)

    b = x.shape[0]
    vg_c = project(xc, mc)
    vg = project(x, m)
    s0 = jnp.zeros((b, h_, 2, dk, dv), F32)
    ocf, ocb, sc = _ret_core(vg_c, None, lg, s0)
    of, ob, _ = _ret_core(vg, tables, lg, sc)
    vw = h_ * dv

    def out(of_, ob_, vg_, t, mm):
        tm = min(512, t.shape[1])
        return _out_res(_ret_prologue, [of_, ob_, vg_],
                        [_row_spec(tm, vw), _row_spec(tm, vw), _row_spec(tm, vw, col=2)],
                        w_out, t, mm[2], tm)

    x = out(of, ob, vg, x, m)
    if want_ctx:
        xc = out(ocf, ocb, vg_c, xc, mc)
    return x, xc


def _ffn(x, m, norm_g, w_in, conv_w, conv_b, w_out, final_g=None):
    return _convglu(x, norm_g, m[4], m[3], w_in.astype(BF16), conv_w, conv_b, w_out.astype(BF16),
                    m[5], final_g)


def kernel(x, c, ctx, c_ctx, mod_w, mod_b, norm1_g, norm2_g, att_w_in, att_q_g, att_k_g, att_w_out, gdn_w_in, gdn_conv_w, gdn_A_log, gdn_dt_bias, gdn_norm_g, gdn_w_out, ret_w_in, ret_decay, ret_w_out, ffn_w_in, ffn_conv_w, ffn_conv_b, ffn_w_out, final_g):
    b, n, d = x.shape
    depth = mod_w.shape[0]
    cc = jnp.concatenate([c, c_ctx[None, :], jnp.zeros((8 - b - 1, d), F32)], axis=0)
    mods = _modulation(cc, mod_w, mod_b)
    tab_att = _rope_tables(n, ATT_HEAD_DIM)
    tab_ret = _rope_tables(n, RET_QK_DIM)
    xc = ctx
    for i in range(depth):
        last = i == depth - 1
        mi = mods[i].reshape(8, 6, d)
        m = [mi[:b, s][:, None, :] for s in range(6)]
        mc = [jnp.broadcast_to(mi[b, s][None, None, :], (b, 1, d)) for s in range(6)]
        kind, j = i % N_MIXERS, i // N_MIXERS
        if kind == 0:
            x, xc = _attention_layer(x, xc, m, mc, norm1_g[i], att_w_in[j], att_q_g[j], att_k_g[j],
                                     att_w_out[j], tab_att, not last)
        elif kind == 1:
            x, xc = _gdn_layer(x, xc, m, mc, norm1_g[i], gdn_w_in[j], gdn_conv_w[j], gdn_A_log[j],
                               gdn_dt_bias[j], gdn_norm_g[j], gdn_w_out[j], not last)
        else:
            x, xc = _retention_layer(x, xc, m, mc, norm1_g[i], ret_w_in[j], ret_decay[j],
                                     ret_w_out[j], tab_ret, not last)
        x = _ffn(x, m, norm2_g[i], ffn_w_in[i], ffn_conv_w[i], ffn_conv_b[i], ffn_w_out[i],
                 final_g if last else None)
        if not last:
            xc = _ffn(xc, mc, norm2_g[i], ffn_w_in[i], ffn_conv_w[i], ffn_conv_b[i], ffn_w_out[i])
    return x
```
